```python
import jax, jax.numpy as jnp
from jax import lax
import numpy as np

D_MODEL = 1024
BATCH = 4
SEQ = 8192
DEPTH = 4

GRID_W = 64
CTX_LEN = 256

N_HEADS = 8
QK_NOPE = 64
QK_ROPE = 32
QK_HEAD = QK_NOPE + QK_ROPE
V_HEAD = 64
Q_LORA = 384
KV_LORA = 256
Q_BLOCK = 128
ATTN_SCALE = QK_HEAD ** -0.5
ROPE_BASE = 10000.0
ROPE_PAIRS = QK_ROPE // 4

LRU_WIDTH = 512
LRU_BLOCKS = 8
LRU_BLOCK = LRU_WIDTH // LRU_BLOCKS
CONV_W = 4
CONV_PAD_LEFT = 2
RG_C = 8.0

D_FF = 2816

N_MOD = 9
EPS = 1e-6

IN_SPLITS = (Q_LORA, Q_LORA + KV_LORA, Q_LORA + KV_LORA + QK_ROPE,
             Q_LORA + KV_LORA + QK_ROPE + LRU_WIDTH, Q_LORA + KV_LORA + QK_ROPE + 2 * LRU_WIDTH)
D_IN = Q_LORA + KV_LORA + QK_ROPE + 2 * LRU_WIDTH + 2 * D_MODEL

kernel_name = 'hybrid_mla_rglru_macaron_dit'


def rms_norm(x, gain=None):
    xf = x.astype(jnp.float32)
    y = xf * lax.rsqrt(jnp.mean(jnp.square(xf), axis=-1, keepdims=True) + EPS)
    if gain is not None:
        y = y * gain.astype(jnp.float32)
    return y.astype(x.dtype)


def modulate(xn, shift, scale):
    return xn * (1.0 + scale) + shift


def swiglu(x, w_in, w_out):
    g, u = jnp.split(x @ w_in, 2, axis=-1)
    return (jax.nn.silu(g) * u) @ w_out


def axial_rope_tables(n_tokens):
    rows = n_tokens // GRID_W
    row = jnp.repeat(jnp.arange(rows, dtype=jnp.float32), GRID_W)
    col = jnp.tile(jnp.arange(GRID_W, dtype=jnp.float32), rows)
    inv_freq = ROPE_BASE ** (-jnp.arange(ROPE_PAIRS, dtype=jnp.float32) / ROPE_PAIRS)
    ang = jnp.concatenate([row[:, None] * inv_freq, col[:, None] * inv_freq], axis=-1)
    return jnp.cos(ang), jnp.sin(ang)


def apply_rope_2d(x, cos, sin):
    shp = x.shape
    xr = x.reshape(shp[:-1] + (2, 2, ROPE_PAIRS))
    x1, x2 = xr[..., 0, :], xr[..., 1, :]
    c = cos.reshape(-1, 1, 2, ROPE_PAIRS).astype(x.dtype)
    s = sin.reshape(-1, 1, 2, ROPE_PAIRS).astype(x.dtype)
    out = jnp.stack([x1 * c - x2 * s, x1 * s + x2 * c], axis=-2)
    return out.reshape(shp)


def rope_part(t, rope):
    if rope is None:
        return t
    return jnp.concatenate([t[..., :QK_NOPE], apply_rope_2d(t[..., QK_NOPE:], *rope)], axis=-1)


def mla_queries(q_a, p, rope):
    B, T, _ = q_a.shape
    q = (rms_norm(q_a, p['q_a_norm']) @ p['w_uq']).reshape(B, T, N_HEADS, QK_HEAD)
    q = rope_part(rms_norm(q, p['q_norm']), rope)
    return q.transpose(0, 2, 1, 3)


def mla_keys_values(kv_a, k_r, p, rope):
    B, T, _ = kv_a.shape
    kv = (rms_norm(kv_a, p['kv_a_norm']) @ p['w_ukv']).reshape(B, T, N_HEADS, QK_NOPE + V_HEAD)
    k_nope, v = kv[..., :QK_NOPE], kv[..., QK_NOPE:]
    k_shared = jnp.broadcast_to(k_r[:, :, None, :], (B, T, N_HEADS, QK_ROPE))
    k = rope_part(rms_norm(jnp.concatenate([k_nope, k_shared], axis=-1), p['k_norm']), rope)
    return k.transpose(0, 2, 1, 3), v.transpose(0, 2, 1, 3)


def attend(q, k, v):
    s = jnp.einsum('bhqd,bhkd->bhqk', q, k).astype(jnp.float32) * ATTN_SCALE
    return jnp.einsum('bhqk,bhkd->bhqd', jax.nn.softmax(s, axis=-1).astype(v.dtype), v)


def latent_attention(q, k, v):
    B, H, S, _ = q.shape
    nb = S // Q_BLOCK
    qb = q.reshape(B, H, nb, Q_BLOCK, QK_HEAD).transpose(2, 0, 1, 3, 4)
    out = lax.map(lambda qblk: attend(qblk, k, v), qb)
    return out.transpose(1, 0, 3, 2, 4).reshape(B, S, N_HEADS * V_HEAD)


def short_conv(x, w, b):
    T = x.shape[1]
    xp = jnp.pad(x, ((0, 0), (CONV_PAD_LEFT, CONV_W - 1 - CONV_PAD_LEFT), (0, 0)))
    y = b
    for k in range(CONV_W):
        y = y + xp[:, k:k + T] * w[k]
    return y


def rglru_coeffs(x, lam, w_r, b_r, w_i, b_i):
    B, T, W = x.shape
    xb = x.reshape(B, T, LRU_BLOCKS, LRU_BLOCK)
    r = jax.nn.sigmoid(jnp.einsum('btgi,gij->btgj', xb, w_r).reshape(B, T, W) + b_r)
    i = jax.nn.sigmoid(jnp.einsum('btgi,gij->btgj', xb, w_i).reshape(B, T, W) + b_i)
    log_a = -RG_C * r.astype(jnp.float32) * jax.nn.softplus(-lam.astype(jnp.float32))
    a = jnp.exp(log_a)
    b = jnp.sqrt(-jnp.expm1(2.0 * log_a)) * (i * x).astype(jnp.float32)
    return a, b


def _lin_combine(e1, e2):
    a1, b1 = e1
    a2, b2 = e2
    return a1 * a2, a2 * b1 + b2


def linear_scan(a, b, h0, reverse):
    if h0 is not None:
        edge = -1 if reverse else 0
        b = b.at[:, edge].add(a[:, edge] * h0)
    _, h = lax.associative_scan(_lin_combine, (a, b), axis=1, reverse=reverse)
    return h


def scan_direction(xc, xl, p, d, reverse):
    prm = (p['lru_lambda'][d], p['w_rgate'][d], p['b_rgate'][d], p['w_igate'][d], p['b_igate'][d])
    hc = linear_scan(*rglru_coeffs(xc, *prm), None, reverse)
    h0 = hc[:, 0] if reverse else hc[:, -1]
    hl = linear_scan(*rglru_coeffs(xl, *prm), h0, reverse)
    return hc, hl


def rglru_mixer(ux_c, uy_c, ux_l, uy_l, p, last):
    xc = short_conv(ux_c, p['conv_w'], p['conv_b'])
    xl = short_conv(ux_l, p['conv_w'], p['conv_b'])
    hc_f, hl_f = scan_direction(xc, xl, p, 0, False)
    hc_b, hl_b = scan_direction(xc, xl, p, 1, True)
    out_l = (hl_f + hl_b).astype(ux_l.dtype) * jax.nn.gelu(uy_l)
    out_c = None if last else (hc_f + hc_b).astype(ux_c.dtype) * jax.nn.gelu(uy_c)
    return out_l, out_c


def merge_branches(att, lru, gate_logits, p):
    g_att, g_lru = jnp.split(jax.nn.sigmoid(gate_logits), 2, axis=-1)
    return (g_att * (att @ p['w_o_attn']) + g_lru * (lru @ p['w_o_lru'])) @ p['w_out']


def hybrid_layer(xl, xc, mod_l, mod_c, rope, p, last):
    sh1_l, sc1_l, g1_l, sh2_l, sc2_l, g2_l, sh3_l, sc3_l, g3_l = jnp.split(mod_l, N_MOD, axis=-1)
    sh1_c, sc1_c, g1_c, sh2_c, sc2_c, g2_c, sh3_c, sc3_c, g3_c = jnp.split(mod_c, N_MOD, axis=-1)
    B, C, _ = xc.shape
    xl = xl + 0.5 * g1_l * swiglu(modulate(rms_norm(xl), sh1_l, sc1_l), p['ff1_w_in'], p['ff1_w_out'])
    xc = xc + 0.5 * g1_c * swiglu(modulate(rms_norm(xc), sh1_c, sc1_c), p['ff1_w_in'], p['ff1_w_out'])
    hl = modulate(rms_norm(xl), sh2_l, sc2_l) @ p['w_in']
    hc = modulate(rms_norm(xc), sh2_c, sc2_c) @ p['w_in']
    qa_l, kva_l, kr_l, ux_l, uy_l, gate_l = jnp.split(hl, IN_SPLITS, axis=-1)
    qa_c, kva_c, kr_c, ux_c, uy_c, gate_c = jnp.split(hc, IN_SPLITS, axis=-1)
    k_c, v_c = mla_keys_values(kva_c, kr_c, p, None)
    k_l, v_l = mla_keys_values(kva_l, kr_l, p, rope)
    q_l = mla_queries(qa_l, p, rope)
    att_l = latent_attention(q_l, jnp.concatenate([k_l, k_c], axis=2), jnp.concatenate([v_l, v_c], axis=2))
    lru_l, lru_c = rglru_mixer(ux_c, uy_c, ux_l, uy_l, p, last)
    xl = xl + g2_l * merge_branches(att_l, lru_l, gate_l, p)
    xl = xl + 0.5 * g3_l * swiglu(modulate(rms_norm(xl), sh3_l, sc3_l), p['ff2_w_in'], p['ff2_w_out'])
    if not last:
        q_c = mla_queries(qa_c, p, None)
        att_c = attend(q_c, k_c, v_c).transpose(0, 2, 1, 3).reshape(B, C, N_HEADS * V_HEAD)
        xc = xc + g2_c * merge_branches(att_c, lru_c, gate_c, p)
        xc = xc + 0.5 * g3_c * swiglu(modulate(rms_norm(xc), sh3_c, sc3_c), p['ff2_w_in'], p['ff2_w_out'])
    return xl, xc


def setup_inputs(seed: int = 0) -> dict:
    key = jax.random.key(seed)
    ks = jax.random.split(key, 32)
    f32 = jnp.float32

    def nrm(k, shape, fan_in):
        return jax.random.normal(k, shape, f32) * fan_in ** -0.5

    def small(k, shape):
        return 0.01 * jax.random.normal(k, shape, f32)

    def gain(k, shape):
        return 1.0 + 0.01 * jax.random.normal(k, shape, f32)

    a0 = jax.random.uniform(ks[20], (DEPTH, 2, LRU_WIDTH), f32, minval=0.9, maxval=0.999)
    s0 = a0 ** (1.0 / RG_C)
    lru_lambda = jnp.log(s0) - jnp.log1p(-s0)
    return {
        'x': jax.random.normal(ks[0], (BATCH, SEQ, D_MODEL), f32),
        'c': jax.random.normal(ks[1], (BATCH, D_MODEL), f32),
        'ctx': jax.random.normal(ks[2], (BATCH, CTX_LEN, D_MODEL), f32),
        'c_ctx': jax.random.normal(ks[3], (D_MODEL,), f32),
        'w_ada': nrm(ks[4], (DEPTH, D_MODEL, N_MOD * D_MODEL), D_MODEL),
        'b_ada': small(ks[5], (DEPTH, N_MOD * D_MODEL)),
        'ff1_w_in': nrm(ks[6], (DEPTH, D_MODEL, 2 * D_FF), D_MODEL),
        'ff1_w_out': nrm(ks[7], (DEPTH, D_FF, D_MODEL), D_FF),
        'ff2_w_in': nrm(ks[8], (DEPTH, D_MODEL, 2 * D_FF), D_MODEL),
        'ff2_w_out': nrm(ks[9], (DEPTH, D_FF, D_MODEL), D_FF),
        'w_in': nrm(ks[10], (DEPTH, D_MODEL, D_IN), D_MODEL),
        'q_a_norm': gain(ks[11], (DEPTH, Q_LORA)),
        'w_uq': nrm(ks[12], (DEPTH, Q_LORA, N_HEADS * QK_HEAD), Q_LORA),
        'kv_a_norm': gain(ks[13], (DEPTH, KV_LORA)),
        'w_ukv': nrm(ks[14], (DEPTH, KV_LORA, N_HEADS * (QK_NOPE + V_HEAD)), KV_LORA),
        'q_norm': gain(ks[15], (DEPTH, QK_HEAD)),
        'k_norm': gain(ks[16], (DEPTH, QK_HEAD)),
        'conv_w': nrm(ks[17], (DEPTH, CONV_W, LRU_WIDTH), CONV_W),
        'conv_b': small(ks[18], (DEPTH, LRU_WIDTH)),
        'lru_lambda': lru_lambda,
        'w_rgate': nrm(ks[21], (DEPTH, 2, LRU_BLOCKS, LRU_BLOCK, LRU_BLOCK), LRU_BLOCK),
        'b_rgate': small(ks[22], (DEPTH, 2, LRU_WIDTH)),
        'w_igate': nrm(ks[23], (DEPTH, 2, LRU_BLOCKS, LRU_BLOCK, LRU_BLOCK), LRU_BLOCK),
        'b_igate': small(ks[24], (DEPTH, 2, LRU_WIDTH)),
        'w_o_attn': nrm(ks[25], (DEPTH, N_HEADS * V_HEAD, D_MODEL), N_HEADS * V_HEAD),
        'w_o_lru': nrm(ks[26], (DEPTH, LRU_WIDTH, D_MODEL), LRU_WIDTH),
        'w_out': nrm(ks[27], (DEPTH, D_MODEL, D_MODEL), D_MODEL),
    }


def reference(x, c, ctx, c_ctx, w_ada, b_ada, ff1_w_in, ff1_w_out, ff2_w_in, ff2_w_out, w_in,
              q_a_norm, w_uq, kv_a_norm, w_ukv, q_norm, k_norm, conv_w, conv_b, lru_lambda,
              w_rgate, b_rgate, w_igate, b_igate, w_o_attn, w_o_lru, w_out):
    rope = axial_rope_tables(x.shape[1])
    cond_l = jax.nn.silu(c)
    cond_c = jax.nn.silu(c_ctx)[None, :]
    xl, xc = x, ctx
    for l in range(DEPTH):
        p = {
            'ff1_w_in': ff1_w_in[l], 'ff1_w_out': ff1_w_out[l],
            'ff2_w_in': ff2_w_in[l], 'ff2_w_out': ff2_w_out[l],
            'w_in': w_in[l],
            'q_a_norm': q_a_norm[l], 'w_uq': w_uq[l],
            'kv_a_norm': kv_a_norm[l], 'w_ukv': w_ukv[l],
            'q_norm': q_norm[l], 'k_norm': k_norm[l],
            'conv_w': conv_w[l], 'conv_b': conv_b[l],
            'lru_lambda': lru_lambda[l], 'w_rgate': w_rgate[l], 'b_rgate': b_rgate[l],
            'w_igate': w_igate[l], 'b_igate': b_igate[l],
            'w_o_attn': w_o_attn[l], 'w_o_lru': w_o_lru[l], 'w_out': w_out[l],
        }
        mod_l = (cond_l @ w_ada[l] + b_ada[l])[:, None, :]
        mod_c = (cond_c @ w_ada[l] + b_ada[l])[:, None, :]
        xl, xc = hybrid_layer(xl, xc, mod_l, mod_c, rope, p, l == DEPTH - 1)
    return xl
```

```python
import functools
import math

import jax
import jax.numpy as jnp
from jax import lax
from jax.experimental import pallas as pl
from jax.experimental.pallas import tpu as pltpu

F32 = jnp.float32
BF16 = jnp.bfloat16

D_MODEL = 1024
N_HEADS = 8
QK_NOPE = 64
QK_ROPE = 32
QK_HEAD = QK_NOPE + QK_ROPE
V_HEAD = 64
Q_LORA = 384
KV_LORA = 256
LRU_WIDTH = 512
LRU_BLOCKS = 8
LRU_BLOCK = LRU_WIDTH // LRU_BLOCKS
CONV_W = 4
RG_C = 8.0
D_FF = 2816
N_MOD = 9
EPS = 1e-6
GRID_W = 64
ROPE_PAIRS = QK_ROPE // 4
ROPE_BASE = 10000.0
ATTN_SCALE = QK_HEAD ** -0.5

LANES = 128
SUBLANES = 8
MXU_DIM = 256
VMEM_LIMIT_BYTES = 60 * 1024 * 1024

HEAD_PAD = LANES
KV_TILE = MXU_DIM
PROJ_A = 768
PROJ_B = 2 * LRU_WIDTH + 2 * D_MODEL
ADA_TN = 1152

NT_DIMS = (((1,), (1,)), ((), ()))
TN_DIMS = (((0,), (0,)), ((), ()))


def _dot(a, b):
    return jnp.dot(a, b, preferred_element_type=F32)


def _rms_rows(x):
    return x * lax.rsqrt(jnp.mean(x * x, axis=-1, keepdims=True) + EPS)


def _cparams(sem):
    return pltpu.CompilerParams(dimension_semantics=sem, vmem_limit_bytes=VMEM_LIMIT_BYTES)


def _const_spec(shape):
    nd = len(shape)
    return pl.BlockSpec(shape, lambda *_: (0,) * nd, pipeline_mode=pl.Buffered(1))


def _ada_kernel(c_ref, w_ref, b_ref, o_ref):
    c = c_ref[...]
    cond = c * jax.nn.sigmoid(c)
    o_ref[0] = jnp.dot(cond, w_ref[0], preferred_element_type=F32,
                       precision=lax.Precision.HIGHEST) + b_ref[0]


def _ada_call(cond_raw, w_ada, b_ada):
    depth, d, n = w_ada.shape
    rows = cond_raw.shape[0]
    return pl.pallas_call(
        _ada_kernel,
        grid=(depth, n // ADA_TN),
        in_specs=[
            pl.BlockSpec((rows, d), lambda l, j: (0, 0)),
            pl.BlockSpec((1, d, ADA_TN), lambda l, j: (l, 0, j)),
            pl.BlockSpec((1, 1, ADA_TN), lambda l, j: (l, 0, j)),
        ],
        out_specs=pl.BlockSpec((1, rows, ADA_TN), lambda l, j: (l, 0, j)),
        out_shape=jax.ShapeDtypeStruct((depth, rows, n), F32),
        compiler_params=_cparams(("arbitrary", "arbitrary")),
        name="ada",
    )(cond_raw, w_ada, b_ada.reshape(depth, 1, n))


def _mod_spec(mod_row):
    if mod_row is None:
        return pl.BlockSpec((1, 1, D_MODEL), lambda b, i: (b, 0, 0))
    return pl.BlockSpec((1, 1, D_MODEL), lambda b, i: (mod_row, 0, 0))


def _ffn_kernel(x_ref, sh_ref, sc_ref, g_ref, win_ref, wout_ref, o_ref):
    x = x_ref[0]
    xb = (_rms_rows(x) * (1.0 + sc_ref[0]) + sh_ref[0]).astype(BF16)
    g = _dot(xb, win_ref[:, :D_FF])
    u = _dot(xb, win_ref[:, D_FF:])
    a = (g * jax.nn.sigmoid(g) * u).astype(BF16)
    y = _dot(a, wout_ref[...])
    o_ref[0] = x + (0.5 * g_ref[0]) * y


def _ffn_call(x, sh, sc, g, w_in, w_out, mod_row, tm):
    b, t, d = x.shape
    return pl.pallas_call(
        _ffn_kernel,
        grid=(b, t // tm),
        in_specs=[
            pl.BlockSpec((1, tm, d), lambda bi, i: (bi, i, 0)),
            _mod_spec(mod_row), _mod_spec(mod_row), _mod_spec(mod_row),
            _const_spec(w_in.shape), _const_spec(w_out.shape),
        ],
        out_specs=pl.BlockSpec((1, tm, d), lambda bi, i: (bi, i, 0)),
        out_shape=jax.ShapeDtypeStruct(x.shape, F32),
        compiler_params=_cparams(("arbitrary", "arbitrary")),
        name="ffn",
    )(x, sh, sc, g, w_in, w_out)


def _rope_t(t, cos, sin):
    p = ROPE_PAIRS
    x1r, x2r, x1c, x2c = t[0:p], t[p:2 * p], t[2 * p:3 * p], t[3 * p:4 * p]
    cr, cc = cos[0:p], cos[p:2 * p]
    sr, sc = sin[0:p], sin[p:2 * p]
    return jnp.concatenate(
        [x1r * cr - x2r * sr, x1r * sr + x2r * cr, x1c * cc - x2c * sc, x1c * sc + x2c * cc], axis=0)


def _proj_kernel(x_ref, sh_ref, sc_ref, wa_ref, wb_ref, gqa_ref, gkva_ref, wuq_ref, wukv_ref,
                 gq_ref, gk_ref, cos_ref, sin_ref,
                 qt_ref, k_ref, v_ref, ux_ref, uy_ref, gs_ref):
    tm = x_ref.shape[1]
    x = x_ref[0]
    xb = (_rms_rows(x) * (1.0 + sc_ref[0]) + sh_ref[0]).astype(BF16)
    ha = _dot(xb, wa_ref[...])
    hb = _dot(xb, wb_ref[...])
    ux_ref[0] = hb[:, :LRU_WIDTH]
    uy_ref[0] = hb[:, LRU_WIDTH:2 * LRU_WIDTH]
    gs_ref[0] = jax.nn.sigmoid(hb[:, 2 * LRU_WIDTH:]).astype(BF16)

    qn = (_rms_rows(ha[:, :Q_LORA]) * gqa_ref[...]).astype(BF16)
    kvn = (_rms_rows(ha[:, Q_LORA:Q_LORA + KV_LORA]) * gkva_ref[...]).astype(BF16)
    q_t = lax.dot_general(wuq_ref[...], qn, NT_DIMS, preferred_element_type=F32)
    kv_t = lax.dot_general(wukv_ref[...], kvn, NT_DIMS, preferred_element_type=F32)
    kr_t = ha[:, Q_LORA + KV_LORA:].T[:QK_ROPE]

    cos = cos_ref[...]
    sin = sin_ref[...]
    gq = gq_ref[...]
    gk = gk_ref[...]
    kr_rot = _rope_t(kr_t * gk[QK_NOPE:], cos, sin)
    kr_ss = jnp.sum(kr_t * kr_t, axis=0, keepdims=True)
    zpad = jnp.zeros((HEAD_PAD - QK_HEAD, tm), F32)
    inv_d = 1.0 / QK_HEAD
    for h in range(N_HEADS):
        q = q_t[QK_HEAD * h:QK_HEAD * (h + 1)]
        rq = lax.rsqrt(jnp.sum(q * q, axis=0, keepdims=True) * inv_d + EPS)
        qs = q * rq * gq
        qh = jnp.concatenate([qs[:QK_NOPE], _rope_t(qs[QK_NOPE:], cos, sin), zpad], axis=0)
        qt_ref[0, HEAD_PAD * h:HEAD_PAD * (h + 1), :] = qh.astype(BF16)

        kn = kv_t[QK_NOPE * h:QK_NOPE * (h + 1)]
        rk = lax.rsqrt((jnp.sum(kn * kn, axis=0, keepdims=True) + kr_ss) * inv_d + EPS)
        kh = jnp.concatenate([kn * gk[:QK_NOPE] * rk, kr_rot * rk, zpad], axis=0)
        k_ref[0, h] = kh.T.astype(BF16)

        v0 = N_HEADS * QK_NOPE + V_HEAD * h
        vt = kv_t[v0:v0 + V_HEAD].astype(BF16)
        for c in range(tm // KV_TILE):
            v_ref[0, h, c] = vt[:, KV_TILE * c:KV_TILE * (c + 1)]


def _proj_call(x, sh, sc, p, cos_t, sin_t, mod_row, tm):
    b, t, d = x.shape
    nkv = t // KV_TILE
    out_shape = (
        jax.ShapeDtypeStruct((b, N_HEADS * HEAD_PAD, t), BF16),
        jax.ShapeDtypeStruct((b, N_HEADS, t, HEAD_PAD), BF16),
        jax.ShapeDtypeStruct((b, N_HEADS, nkv, V_HEAD, KV_TILE), BF16),
        jax.ShapeDtypeStruct((b, t, LRU_WIDTH), F32),
        jax.ShapeDtypeStruct((b, t, LRU_WIDTH), F32),
        jax.ShapeDtypeStruct((b, t, 2 * D_MODEL), BF16),
    )
    out_specs = (
        pl.BlockSpec((1, N_HEADS * HEAD_PAD, tm), lambda bi, i: (bi, 0, i)),
        pl.BlockSpec((1, N_HEADS, tm, HEAD_PAD), lambda bi, i: (bi, 0, i, 0)),
        pl.BlockSpec((1, N_HEADS, tm // KV_TILE, V_HEAD, KV_TILE), lambda bi, i: (bi, 0, i, 0, 0)),
        pl.BlockSpec((1, tm, LRU_WIDTH), lambda bi, i: (bi, i, 0)),
        pl.BlockSpec((1, tm, LRU_WIDTH), lambda bi, i: (bi, i, 0)),
        pl.BlockSpec((1, tm, 2 * D_MODEL), lambda bi, i: (bi, i, 0)),
    )
    consts = (p['wa'], p['wb'], p['gqa'], p['gkva'], p['wuq_t'], p['wukv_t'], p['gq'], p['gk'])
    return pl.pallas_call(
        _proj_kernel,
        grid=(b, t // tm),
        in_specs=[
            pl.BlockSpec((1, tm, d), lambda bi, i: (bi, i, 0)),
            _mod_spec(mod_row), _mod_spec(mod_row),
            *[_const_spec(c.shape) for c in consts],
            pl.BlockSpec((2 * ROPE_PAIRS, tm), lambda bi, i: (0, i)),
            pl.BlockSpec((2 * ROPE_PAIRS, tm), lambda bi, i: (0, i)),
        ],
        out_specs=out_specs,
        out_shape=out_shape,
        compiler_params=_cparams(("arbitrary", "arbitrary")),
        name="proj",
    )(x, sh, sc, *consts, cos_t, sin_t)


def _attn_kernel(*refs, n_tiles):
    q_ref = refs[0]
    o_ref = refs[-1]
    kv_refs = refs[1:-1]
    q_t = q_ref[0]
    tq = q_t.shape[1]

    def step(kt, vt, carry):
        m, l, acc = carry
        s = _dot(kt, q_t)
        m_new = jnp.maximum(m, jnp.max(s, axis=0, keepdims=True))
        p = jnp.exp(s - m_new)
        alpha = jnp.exp(m - m_new)
        l = alpha * l + jnp.sum(p, axis=0, keepdims=True)
        acc = alpha * acc + _dot(vt, p.astype(BF16))
        return m_new, l, acc

    carry = (jnp.full((1, tq), -jnp.inf, F32), jnp.zeros((1, tq), F32), jnp.zeros((V_HEAD, tq), F32))
    for src, n in enumerate(n_tiles):
        k_ref, v_ref = kv_refs[2 * src], kv_refs[2 * src + 1]
        if n == 1:
            carry = step(k_ref[0, 0], v_ref[0, 0, 0], carry)
        else:
            def body(j, c, k_ref=k_ref, v_ref=v_ref):
                off = pl.multiple_of(j * KV_TILE, KV_TILE)
                return step(k_ref[0, 0, pl.ds(off, KV_TILE), :], v_ref[0, 0, j], c)
            carry = lax.fori_loop(0, n, body, carry)
    _, l, acc = carry
    o_ref[0] = (acc / l).astype(BF16)


def _attn_call(q_t, kv_sources, tq):
    b, hp, t = q_t.shape
    in_specs = [pl.BlockSpec((1, HEAD_PAD, tq), lambda bi, h, i: (bi, h, i))]
    args = [q_t]
    n_tiles = []
    for k, v in kv_sources:
        tk = k.shape[2]
        n_tiles.append(tk // KV_TILE)
        in_specs.append(pl.BlockSpec((1, 1, tk, HEAD_PAD), lambda bi, h, i: (bi, h, 0, 0)))
        in_specs.append(pl.BlockSpec((1, 1, tk // KV_TILE, V_HEAD, KV_TILE), lambda bi, h, i: (bi, h, 0, 0, 0)))
        args += [k, v]
    return pl.pallas_call(
        functools.partial(_attn_kernel, n_tiles=tuple(n_tiles)),
        grid=(b, N_HEADS, t // tq),
        in_specs=in_specs,
        out_specs=pl.BlockSpec((1, V_HEAD, tq), lambda bi, h, i: (bi, h, i)),
        out_shape=jax.ShapeDtypeStruct((b, N_HEADS * V_HEAD, t), BF16),
        compiler_params=_cparams(("arbitrary", "arbitrary", "arbitrary")),
        name="attn",
    )(*args)


def _gelu_tanh(x):
    return 0.5 * x * (1.0 + jnp.tanh(math.sqrt(2.0 / math.pi) * (x + 0.044715 * (x * x * x))))


def _lru_kernel(*refs, reverse, n_chunks, chunk, combine):
    (uxc_ref, ux_ref, uxp_ref, uxn_ref, cw_ref, cb_ref, wg_ref, bg_ref, lam_ref) = refs[:9]
    if combine:
        hbc_ref, hbl_ref, uyc_ref, uy_ref = refs[9:13]
        outc_ref, outl_ref, xbuf, a_s, b_s, h_s, carry_s = refs[13:]
    else:
        outc_ref, outl_ref, xbuf, a_s, b_s, h_s, carry_s = refs[9:]
    width = LRU_WIDTH
    halo = SUBLANES
    s = pl.program_id(1)
    is_ctx = s == 0
    c = jnp.clip((n_chunks - s) if reverse else (s - 1), 0, n_chunks - 1)

    zero_h = jnp.zeros((halo, width), F32)
    xbuf[0:halo] = jnp.where(is_ctx | (c == 0), zero_h, uxp_ref[0])
    xbuf[halo:halo + chunk] = jnp.where(is_ctx, uxc_ref[0], ux_ref[0])
    xbuf[halo + chunk:2 * halo + chunk] = jnp.where(is_ctx | (c == n_chunks - 1), zero_h, uxn_ref[0])
    cw = cw_ref[...]
    xc = cb_ref[...]
    for k in range(CONV_W):
        o = halo - 2 + k
        xc = xc + cw[k:k + 1] * xbuf[o:o + chunk]

    xcb = xc.astype(BF16)
    half = width // 2
    d0 = _dot(xcb[:, :half], wg_ref[0])
    d1 = _dot(xcb[:, half:], wg_ref[1])
    bg = bg_ref[...]
    r = jax.nn.sigmoid(jnp.concatenate([d0[:, :half], d1[:, :half]], axis=1) + bg[:, :width])
    i = jax.nn.sigmoid(jnp.concatenate([d0[:, half:], d1[:, half:]], axis=1) + bg[:, width:])
    nl = -lam_ref[...]
    softplus = jnp.maximum(nl, 0.0) + jnp.log(1.0 + jnp.exp(-jnp.abs(nl)))
    a = jnp.exp(-RG_C * r * softplus)
    a_s[...] = a
    b_s[...] = jnp.sqrt(1.0 - a * a) * (i * xc)

    row = lax.broadcasted_iota(jnp.int32, (SUBLANES, width), 0)
    n_groups = chunk // SUBLANES

    def body(g, carry):
        gg = (n_groups - 1 - g) if reverse else g
        off = pl.multiple_of(gg * SUBLANES, SUBLANES)
        av = a_s[pl.ds(off, SUBLANES), :]
        bv = b_s[pl.ds(off, SUBLANES), :]
        for k in (1, 2, 4):
            shift = (SUBLANES - k) if reverse else k
            valid = (row < SUBLANES - k) if reverse else (row >= k)
            a_sh = jnp.where(valid, pltpu.roll(av, shift, 0), 1.0)
            b_sh = jnp.where(valid, pltpu.roll(bv, shift, 0), 0.0)
            bv = av * b_sh + bv
            av = av * a_sh
        hblk = av * carry + bv
        h_s[pl.ds(off, SUBLANES), :] = hblk
        return hblk[0:1] if reverse else hblk[SUBLANES - 1:SUBLANES]

    carry0 = jnp.where(is_ctx, jnp.zeros((1, width), F32), carry_s[...])
    carry_s[...] = lax.fori_loop(0, n_groups, body, carry0)
    h = h_s[...]

    if combine:
        hb = jnp.where(is_ctx, hbc_ref[0], hbl_ref[0])
        uy = jnp.where(is_ctx, uyc_ref[0], uy_ref[0])
        val = ((h + hb) * _gelu_tanh(uy)).astype(BF16)
    else:
        val = h

    @pl.when(is_ctx)
    def _():
        outc_ref[0] = val

    @pl.when(jnp.logical_not(is_ctx))
    def _():
        outl_ref[0] = val


def _lru_call(ux_c, ux_l, p, direction, reverse, combine_with=None):
    b, s_len, w = ux_l.shape
    c_len = ux_c.shape[1]
    chunk = c_len
    assert s_len % chunk == 0 and chunk % SUBLANES == 0
    n = s_len // chunk
    hpc = chunk // SUBLANES
    n_hblk = s_len // SUBLANES

    def cidx(s):
        return jnp.clip((n - s) if reverse else (s - 1), 0, n - 1)

    ctx_spec = pl.BlockSpec((1, c_len, w), lambda bi, s: (bi, 0, 0))
    lat_spec = pl.BlockSpec((1, chunk, w), lambda bi, s: (bi, cidx(s), 0))
    in_specs = [
        ctx_spec, lat_spec,
        pl.BlockSpec((1, SUBLANES, w), lambda bi, s: (bi, jnp.maximum(cidx(s) * hpc - 1, 0), 0)),
        pl.BlockSpec((1, SUBLANES, w), lambda bi, s: (bi, jnp.minimum((cidx(s) + 1) * hpc, n_hblk - 1), 0)),
    ]
    consts = (p['conv_w'], p['conv_b'], p['wg'][direction], p['bg'][direction], p['lam'][direction])
    in_specs += [_const_spec(cst.shape) for cst in consts]
    args = [ux_c, ux_l, ux_l, ux_l, *consts]
    combine = combine_with is not None
    if combine:
        in_specs += [ctx_spec, lat_spec, ctx_spec, lat_spec]
        args += list(combine_with)
    out_dtype = BF16 if combine else F32
    scratch = [
        pltpu.VMEM((chunk + 2 * SUBLANES, w), F32),
        pltpu.VMEM((chunk, w), F32), pltpu.VMEM((chunk, w), F32), pltpu.VMEM((chunk, w), F32),
        pltpu.VMEM((1, w), F32),
    ]
    return pl.pallas_call(
        functools.partial(_lru_kernel, reverse=reverse, n_chunks=n, chunk=chunk, combine=combine),
        grid=(b, n + 1),
        in_specs=in_specs,
        out_specs=(ctx_spec, lat_spec),
        out_shape=(jax.ShapeDtypeStruct(ux_c.shape, out_dtype), jax.ShapeDtypeStruct(ux_l.shape, out_dtype)),
        scratch_shapes=scratch,
        compiler_params=_cparams(("arbitrary", "arbitrary")),
        name="lru_fwd" if combine else "lru_bwd",
    )(*args)


def _merge_kernel(x_ref, att_ref, lru_ref, gs_ref, g_ref, woa_ref, wol_ref, wout_ref, o_ref):
    att = lax.dot_general(att_ref[0], woa_ref[...], TN_DIMS, preferred_element_type=F32)
    lru = _dot(lru_ref[0], wol_ref[...])
    gs = gs_ref[0].astype(F32)
    merged = (gs[:, :D_MODEL] * att + gs[:, D_MODEL:] * lru).astype(BF16)
    o_ref[0] = x_ref[0] + g_ref[0] * _dot(merged, wout_ref[...])


def _merge_call(x, att_t, lru, gs, g, p, mod_row, tm):
    b, t, d = x.shape
    consts = (p['w_o_attn'], p['w_o_lru'], p['w_out'])
    return pl.pallas_call(
        _merge_kernel,
        grid=(b, t // tm),
        in_specs=[
            pl.BlockSpec((1, tm, d), lambda bi, i: (bi, i, 0)),
            pl.BlockSpec((1, N_HEADS * V_HEAD, tm), lambda bi, i: (bi, 0, i)),
            pl.BlockSpec((1, tm, LRU_WIDTH), lambda bi, i: (bi, i, 0)),
            pl.BlockSpec((1, tm, 2 * D_MODEL), lambda bi, i: (bi, i, 0)),
            _mod_spec(mod_row),
            *[_const_spec(c.shape) for c in consts],
        ],
        out_specs=pl.BlockSpec((1, tm, d), lambda bi, i: (bi, i, 0)),
        out_shape=jax.ShapeDtypeStruct(x.shape, F32),
        compiler_params=_cparams(("arbitrary", "arbitrary")),
        name="merge",
    )(x, att_t, lru, gs, g, *consts)


def _block_diag_gates(w_r, w_i):
    def bd(w4):
        z = jnp.zeros((4, LRU_BLOCK, 4, LRU_BLOCK), w4.dtype)
        idx = jnp.arange(4)
        z = z.at[idx, :, idx, :].set(w4)
        return z.reshape(4 * LRU_BLOCK, 4 * LRU_BLOCK)
    groups = []
    for j in range(2):
        groups.append(jnp.concatenate([bd(w_r[4 * j:4 * j + 4]), bd(w_i[4 * j:4 * j + 4])], axis=1))
    return jnp.stack(groups).astype(BF16)


def _layer_params(l, w):
    w_in = w['w_in'][l]
    split = Q_LORA + KV_LORA + QK_ROPE
    wa = jnp.pad(w_in[:, :split], ((0, 0), (0, PROJ_A - split))).astype(BF16)
    wb = w_in[:, split:].astype(BF16)
    w_ukv = w['w_ukv'][l].reshape(KV_LORA, N_HEADS, QK_NOPE + V_HEAD)
    wukv_t = jnp.concatenate([
        w_ukv[:, :, :QK_NOPE].reshape(KV_LORA, N_HEADS * QK_NOPE),
        w_ukv[:, :, QK_NOPE:].reshape(KV_LORA, N_HEADS * V_HEAD)], axis=1).T.astype(BF16)
    return {
        'ff1_w_in': w['ff1_w_in'][l].astype(BF16), 'ff1_w_out': w['ff1_w_out'][l].astype(BF16),
        'ff2_w_in': w['ff2_w_in'][l].astype(BF16), 'ff2_w_out': w['ff2_w_out'][l].astype(BF16),
        'wa': wa, 'wb': wb,
        'gqa': w['q_a_norm'][l].reshape(1, Q_LORA), 'gkva': w['kv_a_norm'][l].reshape(1, KV_LORA),
        'wuq_t': w['w_uq'][l].T.astype(BF16), 'wukv_t': wukv_t,
        'gq': (w['q_norm'][l] * ATTN_SCALE).reshape(QK_HEAD, 1), 'gk': w['k_norm'][l].reshape(QK_HEAD, 1),
        'conv_w': w['conv_w'][l], 'conv_b': w['conv_b'][l].reshape(1, LRU_WIDTH),
        'wg': [_block_diag_gates(w['w_rgate'][l, d], w['w_igate'][l, d]) for d in range(2)],
        'bg': [jnp.concatenate([w['b_rgate'][l, d], w['b_igate'][l, d]]).reshape(1, 2 * LRU_WIDTH)
               for d in range(2)],
        'lam': [w['lru_lambda'][l, d].reshape(1, LRU_WIDTH) for d in range(2)],
        'w_o_attn': w['w_o_attn'][l].astype(BF16), 'w_o_lru': w['w_o_lru'][l].astype(BF16),
        'w_out': w['w_out'][l].astype(BF16),
    }


def _rope_tables_t(n_tokens):
    rows = n_tokens // GRID_W
    row = jnp.repeat(jnp.arange(rows, dtype=F32), GRID_W)
    col = jnp.tile(jnp.arange(GRID_W, dtype=F32), rows)
    inv_freq = ROPE_BASE ** (-jnp.arange(ROPE_PAIRS, dtype=F32) / ROPE_PAIRS)
    ang = jnp.concatenate([inv_freq[:, None] * row[None, :], inv_freq[:, None] * col[None, :]], axis=0)
    return jnp.cos(ang), jnp.sin(ang)


def kernel(x, c, ctx, c_ctx, w_ada, b_ada, ff1_w_in, ff1_w_out, ff2_w_in, ff2_w_out, w_in, q_a_norm, w_uq,
           kv_a_norm, w_ukv, q_norm, k_norm, conv_w, conv_b, lru_lambda, w_rgate, b_rgate, w_igate, b_igate,
           w_o_attn, w_o_lru, w_out):
    weights = dict(ff1_w_in=ff1_w_in, ff1_w_out=ff1_w_out, ff2_w_in=ff2_w_in, ff2_w_out=ff2_w_out, w_in=w_in,
                   q_a_norm=q_a_norm, w_uq=w_uq, kv_a_norm=kv_a_norm, w_ukv=w_ukv, q_norm=q_norm, k_norm=k_norm,
                   conv_w=conv_w, conv_b=conv_b, lru_lambda=lru_lambda, w_rgate=w_rgate, b_rgate=b_rgate,
                   w_igate=w_igate, b_igate=b_igate, w_o_attn=w_o_attn, w_o_lru=w_o_lru, w_out=w_out)
    batch, seq, d = x.shape
    c_len = ctx.shape[1]
    depth = w_ada.shape[0]
    assert d == D_MODEL and c_len == KV_TILE and seq % KV_TILE == 0 and batch < SUBLANES
    ctx_row = batch
    tm = 256
    tq = 256

    cond_raw = jnp.zeros((SUBLANES, d), F32).at[:batch].set(c).at[ctx_row].set(c_ctx)
    mods = _ada_call(cond_raw, w_ada, b_ada)
    mods = mods.reshape(depth, SUBLANES, N_MOD, 1, d).transpose(0, 2, 1, 3, 4)

    cos_l, sin_l = _rope_tables_t(seq)
    cos_c = jnp.ones((2 * ROPE_PAIRS, c_len), F32)
    sin_c = jnp.zeros((2 * ROPE_PAIRS, c_len), F32)

    xl, xc = x, ctx
    for l in range(depth):
        p = _layer_params(l, weights)
        m = mods[l]
        last = l == depth - 1
        xl = _ffn_call(xl, m[0], m[1], m[2], p['ff1_w_in'], p['ff1_w_out'], None, tm)
        xc = _ffn_call(xc, m[0], m[1], m[2], p['ff1_w_in'], p['ff1_w_out'], ctx_row, c_len)
        qt_l, k_l, v_l, ux_l, uy_l, gs_l = _proj_call(xl, m[3], m[4], p, cos_l, sin_l, None, tm)
        qt_c, k_c, v_c, ux_c, uy_c, gs_c = _proj_call(xc, m[3], m[4], p, cos_c, sin_c, ctx_row, c_len)
        att_l = _attn_call(qt_l, [(k_l, v_l), (k_c, v_c)], tq)
        hb_c, hb_l = _lru_call(ux_c, ux_l, p, 1, True)
        lru_c, lru_l = _lru_call(ux_c, ux_l, p, 0, False, combine_with=(hb_c, hb_l, uy_c, uy_l))
        xl = _merge_call(xl, att_l, lru_l, gs_l, m[5], p, None, tm)
        xl = _ffn_call(xl, m[6], m[7], m[8], p['ff2_w_in'], p['ff2_w_out'], None, tm)
        if not last:
            att_c = _attn_call(qt_c, [(k_c, v_c)], c_len)
            xc = _merge_call(xc, att_c, lru_c, gs_c, m[5], p, ctx_row, c_len)
            xc = _ffn_call(xc, m[6], m[7], m[8], p['ff2_w_in'], p['ff2_w_out'], ctx_row, c_len)
    return xl
```

```python
import functools
import math

import jax
import jax.numpy as jnp
from jax import lax
from jax.experimental import pallas as pl
from jax.experimental.pallas import tpu as pltpu

F32 = jnp.float32
BF16 = jnp.bfloat16

D_MODEL = 1024
N_HEADS = 8
QK_NOPE = 64
QK_ROPE = 32
QK_HEAD = QK_NOPE + QK_ROPE
V_HEAD = 64
Q_LORA = 384
KV_LORA = 256
LRU_WIDTH = 512
LRU_BLOCKS = 8
LRU_BLOCK = LRU_WIDTH // LRU_BLOCKS
CONV_W = 4
RG_C = 8.0
D_FF = 2816
N_MOD = 9
EPS = 1e-6
GRID_W = 64
ROPE_PAIRS = QK_ROPE // 4
ROPE_BASE = 10000.0
ATTN_SCALE = QK_HEAD ** -0.5

LANES = 128
SUBLANES = 8
MXU_DIM = 256
VMEM_LIMIT_BYTES = 60 * 1024 * 1024

HEAD_PAD = LANES
V_AUG = V_HEAD + 16
Q_BLOCK_LANES = MXU_DIM
KV_TILE = MXU_DIM
PROJ_A = 768
PROJ_B = 2 * LRU_WIDTH + 2 * D_MODEL
ADA_TN = 1152

NT_DIMS = (((1,), (1,)), ((), ()))
TN_DIMS = (((0,), (0,)), ((), ()))


def _dot(a, b):
    return jnp.dot(a, b, preferred_element_type=F32)


def _rms_rows(x):
    return x * lax.rsqrt(jnp.mean(x * x, axis=-1, keepdims=True) + EPS)


def _cparams(sem, flags=None):
    return pltpu.CompilerParams(dimension_semantics=sem, vmem_limit_bytes=VMEM_LIMIT_BYTES, flags=flags)


def _const_spec(shape):
    nd = len(shape)
    return pl.BlockSpec(shape, lambda *_: (0,) * nd, pipeline_mode=pl.Buffered(1))


def _ada_kernel(c_ref, w_ref, b_ref, o_ref):
    c = c_ref[...]
    cond = c * jax.nn.sigmoid(c)
    o_ref[0] = jnp.dot(cond, w_ref[0], preferred_element_type=F32,
                       precision=lax.Precision.HIGHEST) + b_ref[0]


def _ada_call(cond_raw, w_ada, b_ada):
    depth, d, n = w_ada.shape
    rows = cond_raw.shape[0]
    return pl.pallas_call(
        _ada_kernel,
        grid=(depth, n // ADA_TN),
        in_specs=[
            pl.BlockSpec((rows, d), lambda l, j: (0, 0)),
            pl.BlockSpec((1, d, ADA_TN), lambda l, j: (l, 0, j)),
            pl.BlockSpec((1, 1, ADA_TN), lambda l, j: (l, 0, j)),
        ],
        out_specs=pl.BlockSpec((1, rows, ADA_TN), lambda l, j: (l, 0, j)),
        out_shape=jax.ShapeDtypeStruct((depth, rows, n), F32),
        compiler_params=_cparams(("arbitrary", "arbitrary")),
        name="ada",
    )(cond_raw, w_ada, b_ada.reshape(depth, 1, n))


def _mod_spec(mod_row):
    if mod_row is None:
        return pl.BlockSpec((1, 1, D_MODEL), lambda b, i: (b, 0, 0))
    return pl.BlockSpec((1, 1, D_MODEL), lambda b, i: (mod_row, 0, 0))


def _ffn_kernel(x_ref, sh_ref, sc_ref, g_ref, win_ref, wout_ref, o_ref):
    x = x_ref[0]
    xb = (_rms_rows(x) * (1.0 + sc_ref[0]) + sh_ref[0]).astype(BF16)
    g = _dot(xb, win_ref[:, :D_FF])
    u = _dot(xb, win_ref[:, D_FF:])
    a = (g * jax.nn.sigmoid(g) * u).astype(BF16)
    y = _dot(a, wout_ref[...])
    o_ref[0] = x + (0.5 * g_ref[0]) * y


def _ffn_call(x, sh, sc, g, w_in, w_out, mod_row, tm):
    b, t, d = x.shape
    return pl.pallas_call(
        _ffn_kernel,
        grid=(b, t // tm),
        in_specs=[
            pl.BlockSpec((1, tm, d), lambda bi, i: (bi, i, 0)),
            _mod_spec(mod_row), _mod_spec(mod_row), _mod_spec(mod_row),
            _const_spec(w_in.shape), _const_spec(w_out.shape),
        ],
        out_specs=pl.BlockSpec((1, tm, d), lambda bi, i: (bi, i, 0)),
        out_shape=jax.ShapeDtypeStruct(x.shape, F32),
        compiler_params=_cparams(("arbitrary", "arbitrary")),
        name="ffn",
    )(x, sh, sc, g, w_in, w_out)


def _rope_t(t, cos, sin):
    p = ROPE_PAIRS
    x1r, x2r, x1c, x2c = t[0:p], t[p:2 * p], t[2 * p:3 * p], t[3 * p:4 * p]
    cr, cc = cos[0:p], cos[p:2 * p]
    sr, sc = sin[0:p], sin[p:2 * p]
    return jnp.concatenate(
        [x1r * cr - x2r * sr, x1r * sr + x2r * cr, x1c * cc - x2c * sc, x1c * sc + x2c * cc], axis=0)


def _proj_kernel(x_ref, sh_ref, sc_ref, wa_ref, wb_ref, gqa_ref, gkva_ref, wuq_ref, wukv_ref,
                 gq_ref, gk_ref, cos_ref, sin_ref,
                 qt_ref, k_ref, v_ref, ux_ref, uy_ref, gs_ref):
    tm = x_ref.shape[1]
    x = x_ref[0]
    xb = (_rms_rows(x) * (1.0 + sc_ref[0]) + sh_ref[0]).astype(BF16)
    ha = _dot(xb, wa_ref[...])
    hb = _dot(xb, wb_ref[...])
    ux_ref[0] = hb[:, :LRU_WIDTH]
    uy_ref[0] = hb[:, LRU_WIDTH:2 * LRU_WIDTH]
    gs_ref[0] = jax.nn.sigmoid(hb[:, 2 * LRU_WIDTH:]).astype(BF16)

    qn = (_rms_rows(ha[:, :Q_LORA]) * gqa_ref[...]).astype(BF16)
    kvn = (_rms_rows(ha[:, Q_LORA:Q_LORA + KV_LORA]) * gkva_ref[...]).astype(BF16)
    q_t = lax.dot_general(wuq_ref[...], qn, NT_DIMS, preferred_element_type=F32)
    kv_t = lax.dot_general(wukv_ref[...], kvn, NT_DIMS, preferred_element_type=F32)
    kr_t = ha[:, Q_LORA + KV_LORA:].T[:QK_ROPE]

    cos = cos_ref[...]
    sin = sin_ref[...]
    gq = gq_ref[...]
    gk = gk_ref[...]
    kr_rot = _rope_t(kr_t * gk[QK_NOPE:], cos, sin)
    kr_ss = jnp.sum(kr_t * kr_t, axis=0, keepdims=True)
    zpad = jnp.zeros((HEAD_PAD - QK_HEAD, tm), F32)
    ones_rows = jnp.where(lax.broadcasted_iota(jnp.int32, (V_AUG - V_HEAD, tm), 0) == 0, 1.0, 0.0)
    inv_d = 1.0 / QK_HEAD
    for h in range(N_HEADS):
        q = q_t[QK_HEAD * h:QK_HEAD * (h + 1)]
        rq = lax.rsqrt(jnp.sum(q * q, axis=0, keepdims=True) * inv_d + EPS)
        qs = q * rq * gq
        qh = jnp.concatenate([qs[:QK_NOPE], _rope_t(qs[QK_NOPE:], cos, sin), zpad], axis=0)
        qt_ref[0, HEAD_PAD * h:HEAD_PAD * (h + 1), :] = qh.astype(BF16)

        kn = kv_t[QK_NOPE * h:QK_NOPE * (h + 1)]
        rk = lax.rsqrt((jnp.sum(kn * kn, axis=0, keepdims=True) + kr_ss) * inv_d + EPS)
        kh = jnp.concatenate([kn * gk[:QK_NOPE] * rk, kr_rot * rk, zpad], axis=0)
        k_ref[0, h] = kh.T.astype(BF16)

        v0 = N_HEADS * QK_NOPE + V_HEAD * h
        vt = jnp.concatenate([kv_t[v0:v0 + V_HEAD], ones_rows], axis=0).astype(BF16)
        for c in range(tm // KV_TILE):
            v_ref[0, h, c] = vt[:, KV_TILE * c:KV_TILE * (c + 1)]


def _proj_call(x, sh, sc, p, cos_t, sin_t, mod_row, tm):
    b, t, d = x.shape
    nkv = t // KV_TILE
    out_shape = (
        jax.ShapeDtypeStruct((b, N_HEADS * HEAD_PAD, t), BF16),
        jax.ShapeDtypeStruct((b, N_HEADS, t, HEAD_PAD), BF16),
        jax.ShapeDtypeStruct((b, N_HEADS, nkv, V_AUG, KV_TILE), BF16),
        jax.ShapeDtypeStruct((b, t, LRU_WIDTH), F32),
        jax.ShapeDtypeStruct((b, t, LRU_WIDTH), F32),
        jax.ShapeDtypeStruct((b, t, 2 * D_MODEL), BF16),
    )
    out_specs = (
        pl.BlockSpec((1, N_HEADS * HEAD_PAD, tm), lambda bi, i: (bi, 0, i)),
        pl.BlockSpec((1, N_HEADS, tm, HEAD_PAD), lambda bi, i: (bi, 0, i, 0)),
        pl.BlockSpec((1, N_HEADS, tm // KV_TILE, V_AUG, KV_TILE), lambda bi, i: (bi, 0, i, 0, 0)),
        pl.BlockSpec((1, tm, LRU_WIDTH), lambda bi, i: (bi, i, 0)),
        pl.BlockSpec((1, tm, LRU_WIDTH), lambda bi, i: (bi, i, 0)),
        pl.BlockSpec((1, tm, 2 * D_MODEL), lambda bi, i: (bi, i, 0)),
    )
    consts = (p['wa'], p['wb'], p['gqa'], p['gkva'], p['wuq_t'], p['wukv_t'], p['gq'], p['gk'])
    return pl.pallas_call(
        _proj_kernel,
        grid=(b, t // tm),
        in_specs=[
            pl.BlockSpec((1, tm, d), lambda bi, i: (bi, i, 0)),
            _mod_spec(mod_row), _mod_spec(mod_row),
            *[_const_spec(c.shape) for c in consts],
            pl.BlockSpec((2 * ROPE_PAIRS, tm), lambda bi, i: (0, i)),
            pl.BlockSpec((2 * ROPE_PAIRS, tm), lambda bi, i: (0, i)),
        ],
        out_specs=out_specs,
        out_shape=out_shape,
        compiler_params=_cparams(("arbitrary", "arbitrary")),
        name="proj",
    )(x, sh, sc, *consts, cos_t, sin_t)


def _col_max(s):
    rows = s.shape[0]
    while rows > 4 * SUBLANES and rows % 2 == 0:
        rows //= 2
        s = jnp.maximum(s[:rows], s[rows:])
    return jnp.max(s, axis=0, keepdims=True)


def _attn_kernel(*refs, n_tiles, unroll):
    n_kv = 2 * len(n_tiles)
    q_ref = refs[0]
    kv_refs = refs[1:1 + n_kv]
    o_ref = refs[1 + n_kv]
    s_scr, mt_scr = refs[2 + n_kv:]
    tq = q_ref.shape[2]
    n_blocks = tq // Q_BLOCK_LANES
    q_blocks = [q_ref[0, :, Q_BLOCK_LANES * i:Q_BLOCK_LANES * (i + 1)] for i in range(n_blocks)]

    def scores(kt, slot):
        for b, q_b in enumerate(q_blocks):
            s = _dot(kt, q_b)
            s_scr[slot, b] = s
            mt_scr[slot, b] = _col_max(s)

    def update(slot, vt, state):
        out = []
        for b, (m, acc) in enumerate(state):
            m_new = jnp.maximum(m, mt_scr[slot, b])
            p = jnp.exp2(s_scr[slot, b] - m_new).astype(BF16)
            alpha = jnp.exp2(m - m_new)
            out.append((m_new, alpha * acc + _dot(vt, p)))
        return tuple(out)

    tiles = []
    for src, n in enumerate(n_tiles):
        k_ref, v_ref = kv_refs[2 * src], kv_refs[2 * src + 1]
        tiles += [(k_ref, v_ref, j) for j in range(n)]
    k0_ref, v0_ref, n0 = kv_refs[0], kv_refs[1], n_tiles[0]
    n_pairs = (n0 - 2) // 2 if len(tiles) > n0 else 0
    n_loop = 2 * n_pairs

    def k_tile(k_ref, j):
        off = j * KV_TILE if isinstance(j, int) else pl.multiple_of(j * KV_TILE, KV_TILE)
        return k_ref[0, 0, pl.ds(off, KV_TILE), :]

    state = tuple((jnp.full((1, Q_BLOCK_LANES), -jnp.inf, F32), jnp.zeros((V_AUG, Q_BLOCK_LANES), F32))
                  for _ in range(n_blocks))
    scores(k_tile(tiles[0][0], 0), 0)
    if n_pairs:
        def body(g, st):
            j = 2 * g
            scores(k_tile(k0_ref, j + 1), 1)
            st = update(0, v0_ref[0, 0, j], st)
            scores(k_tile(k0_ref, j + 2), 0)
            return update(1, v0_ref[0, 0, j + 1], st)
        state = lax.fori_loop(0, n_pairs, body, state, unroll=unroll)
    for i in range(n_loop, len(tiles)):
        slot = (i - n_loop) % 2
        if i + 1 < len(tiles):
            scores(k_tile(tiles[i + 1][0], tiles[i + 1][2]), 1 - slot)
        state = update(slot, tiles[i][1][0, 0, tiles[i][2]], state)
    for i, (_, acc) in enumerate(state):
        o_ref[0, :, Q_BLOCK_LANES * i:Q_BLOCK_LANES * (i + 1)] = (
            acc[:V_HEAD] / acc[V_HEAD:V_HEAD + 1]).astype(BF16)


def _attn_call(q_t, kv_sources, tq, unroll=1):
    b, hp, t = q_t.shape
    in_specs = [pl.BlockSpec((1, HEAD_PAD, tq), lambda bi, h, i: (bi, h, i))]
    args = [q_t]
    n_tiles = []
    for k, v in kv_sources:
        tk = k.shape[2]
        n_tiles.append(tk // KV_TILE)
        in_specs.append(pl.BlockSpec((1, 1, tk, HEAD_PAD), lambda bi, h, i: (bi, h, 0, 0)))
        in_specs.append(pl.BlockSpec((1, 1, tk // KV_TILE, V_AUG, KV_TILE), lambda bi, h, i: (bi, h, 0, 0, 0)))
        args += [k, v]
    return pl.pallas_call(
        functools.partial(_attn_kernel, n_tiles=tuple(n_tiles), unroll=unroll),
        grid=(b, N_HEADS, t // tq),
        in_specs=in_specs,
        out_specs=pl.BlockSpec((1, V_HEAD, tq), lambda bi, h, i: (bi, h, i)),
        out_shape=jax.ShapeDtypeStruct((b, N_HEADS * V_HEAD, t), BF16),
        scratch_shapes=[
            pltpu.VMEM((2, tq // Q_BLOCK_LANES, KV_TILE, Q_BLOCK_LANES), F32),
            pltpu.VMEM((2, tq // Q_BLOCK_LANES, 1, Q_BLOCK_LANES), F32),
        ],
        compiler_params=_cparams(("arbitrary", "arbitrary", "arbitrary")),
        name="attn",
    )(*args)


def _gelu_tanh(x):
    return 0.5 * x * (1.0 + jnp.tanh(math.sqrt(2.0 / math.pi) * (x + 0.044715 * (x * x * x))))


def _lru_kernel(*refs, reverse, n_chunks, chunk, combine):
    (uxc_ref, ux_ref, uxp_ref, uxn_ref, cw_ref, cb_ref, wg_ref, bg_ref, lam_ref) = refs[:9]
    if combine:
        hbc_ref, hbl_ref, uyc_ref, uy_ref = refs[9:13]
        outc_ref, outl_ref, xbuf, a_s, b_s, h_s, carry_s = refs[13:]
    else:
        outc_ref, outl_ref, xbuf, a_s, b_s, h_s, carry_s = refs[9:]
    width = LRU_WIDTH
    halo = SUBLANES
    s = pl.program_id(1)
    is_ctx = s == 0
    c = jnp.clip((n_chunks - s) if reverse else (s - 1), 0, n_chunks - 1)

    zero_h = jnp.zeros((halo, width), F32)
    xbuf[0:halo] = jnp.where(is_ctx | (c == 0), zero_h, uxp_ref[0])
    xbuf[halo:halo + chunk] = jnp.where(is_ctx, uxc_ref[0], ux_ref[0])
    xbuf[halo + chunk:2 * halo + chunk] = jnp.where(is_ctx | (c == n_chunks - 1), zero_h, uxn_ref[0])
    cw = cw_ref[...]
    xc = cb_ref[...]
    for k in range(CONV_W):
        o = halo - 2 + k
        xc = xc + cw[k:k + 1] * xbuf[o:o + chunk]

    xcb = xc.astype(BF16)
    half = width // 2
    d0 = _dot(xcb[:, :half], wg_ref[0])
    d1 = _dot(xcb[:, half:], wg_ref[1])
    bg = bg_ref[...]
    r = jax.nn.sigmoid(jnp.concatenate([d0[:, :half], d1[:, :half]], axis=1) + bg[:, :width])
    i = jax.nn.sigmoid(jnp.concatenate([d0[:, half:], d1[:, half:]], axis=1) + bg[:, width:])
    nl = -lam_ref[...]
    softplus = jnp.maximum(nl, 0.0) + jnp.log(1.0 + jnp.exp(-jnp.abs(nl)))
    a = jnp.exp(-RG_C * r * softplus)
    a_s[...] = a
    b_s[...] = jnp.sqrt(1.0 - a * a) * (i * xc)

    row = lax.broadcasted_iota(jnp.int32, (SUBLANES, width), 0)
    n_groups = chunk // SUBLANES

    def body(g, carry):
        gg = (n_groups - 1 - g) if reverse else g
        off = pl.multiple_of(gg * SUBLANES, SUBLANES)
        av = a_s[pl.ds(off, SUBLANES), :]
        bv = b_s[pl.ds(off, SUBLANES), :]
        for k in (1, 2, 4):
            shift = (SUBLANES - k) if reverse else k
            valid = (row < SUBLANES - k) if reverse else (row >= k)
            a_sh = jnp.where(valid, pltpu.roll(av, shift, 0), 1.0)
            b_sh = jnp.where(valid, pltpu.roll(bv, shift, 0), 0.0)
            bv = av * b_sh + bv
            av = av * a_sh
        hblk = av * carry + bv
        h_s[pl.ds(off, SUBLANES), :] = hblk
        return hblk[0:1] if reverse else hblk[SUBLANES - 1:SUBLANES]

    carry0 = jnp.where(is_ctx, jnp.zeros((1, width), F32), carry_s[...])
    carry_s[...] = lax.fori_loop(0, n_groups, body, carry0)
    h = h_s[...]

    if combine:
        hb = jnp.where(is_ctx, hbc_ref[0], hbl_ref[0])
        uy = jnp.where(is_ctx, uyc_ref[0], uy_ref[0])
        val = ((h + hb) * _gelu_tanh(uy)).astype(BF16)
    else:
        val = h

    @pl.when(is_ctx)
    def _():
        outc_ref[0] = val

    @pl.when(jnp.logical_not(is_ctx))
    def _():
        outl_ref[0] = val


def _lru_call(ux_c, ux_l, p, direction, reverse, combine_with=None):
    b, s_len, w = ux_l.shape
    c_len = ux_c.shape[1]
    chunk = c_len
    assert s_len % chunk == 0 and chunk % SUBLANES == 0
    n = s_len // chunk
    hpc = chunk // SUBLANES
    n_hblk = s_len // SUBLANES

    def cidx(s):
        return jnp.clip((n - s) if reverse else (s - 1), 0, n - 1)

    ctx_spec = pl.BlockSpec((1, c_len, w), lambda bi, s: (bi, 0, 0))
    lat_spec = pl.BlockSpec((1, chunk, w), lambda bi, s: (bi, cidx(s), 0))
    in_specs = [
        ctx_spec, lat_spec,
        pl.BlockSpec((1, SUBLANES, w), lambda bi, s: (bi, jnp.maximum(cidx(s) * hpc - 1, 0), 0)),
        pl.BlockSpec((1, SUBLANES, w), lambda bi, s: (bi, jnp.minimum((cidx(s) + 1) * hpc, n_hblk - 1), 0)),
    ]
    consts = (p['conv_w'], p['conv_b'], p['wg'][direction], p['bg'][direction], p['lam'][direction])
    in_specs += [_const_spec(cst.shape) for cst in consts]
    args = [ux_c, ux_l, ux_l, ux_l, *consts]
    combine = combine_with is not None
    if combine:
        in_specs += [ctx_spec, lat_spec, ctx_spec, lat_spec]
        args += list(combine_with)
    out_dtype = BF16 if combine else F32
    scratch = [
        pltpu.VMEM((chunk + 2 * SUBLANES, w), F32),
        pltpu.VMEM((chunk, w), F32), pltpu.VMEM((chunk, w), F32), pltpu.VMEM((chunk, w), F32),
        pltpu.VMEM((1, w), F32),
    ]
    return pl.pallas_call(
        functools.partial(_lru_kernel, reverse=reverse, n_chunks=n, chunk=chunk, combine=combine),
        grid=(b, n + 1),
        in_specs=in_specs,
        out_specs=(ctx_spec, lat_spec),
        out_shape=(jax.ShapeDtypeStruct(ux_c.shape, out_dtype), jax.ShapeDtypeStruct(ux_l.shape, out_dtype)),
        scratch_shapes=scratch,
        compiler_params=_cparams(("arbitrary", "arbitrary")),
        name="lru_fwd" if combine else "lru_bwd",
    )(*args)


def _merge_kernel(x_ref, att_ref, lru_ref, gs_ref, g_ref, woa_ref, wol_ref, wout_ref, o_ref):
    att = lax.dot_general(att_ref[0], woa_ref[...], TN_DIMS, preferred_element_type=F32)
    lru = _dot(lru_ref[0], wol_ref[...])
    gs = gs_ref[0].astype(F32)
    merged = (gs[:, :D_MODEL] * att + gs[:, D_MODEL:] * lru).astype(BF16)
    o_ref[0] = x_ref[0] + g_ref[0] * _dot(merged, wout_ref[...])


def _merge_call(x, att_t, lru, gs, g, p, mod_row, tm):
    b, t, d = x.shape
    consts = (p['w_o_attn'], p['w_o_lru'], p['w_out'])
    return pl.pallas_call(
        _merge_kernel,
        grid=(b, t // tm),
        in_specs=[
            pl.BlockSpec((1, tm, d), lambda bi, i: (bi, i, 0)),
            pl.BlockSpec((1, N_HEADS * V_HEAD, tm), lambda bi, i: (bi, 0, i)),
            pl.BlockSpec((1, tm, LRU_WIDTH), lambda bi, i: (bi, i, 0)),
            pl.BlockSpec((1, tm, 2 * D_MODEL), lambda bi, i: (bi, i, 0)),
            _mod_spec(mod_row),
            *[_const_spec(c.shape) for c in consts],
        ],
        out_specs=pl.BlockSpec((1, tm, d), lambda bi, i: (bi, i, 0)),
        out_shape=jax.ShapeDtypeStruct(x.shape, F32),
        compiler_params=_cparams(("arbitrary", "arbitrary")),
        name="merge",
    )(x, att_t, lru, gs, g, *consts)


def _block_diag_gates(w_r, w_i):
    def bd(w4):
        z = jnp.zeros((4, LRU_BLOCK, 4, LRU_BLOCK), w4.dtype)
        idx = jnp.arange(4)
        z = z.at[idx, :, idx, :].set(w4)
        return z.reshape(4 * LRU_BLOCK, 4 * LRU_BLOCK)
    groups = []
    for j in range(2):
        groups.append(jnp.concatenate([bd(w_r[4 * j:4 * j + 4]), bd(w_i[4 * j:4 * j + 4])], axis=1))
    return jnp.stack(groups).astype(BF16)


def _layer_params(l, w):
    w_in = w['w_in'][l]
    split = Q_LORA + KV_LORA + QK_ROPE
    wa = jnp.pad(w_in[:, :split], ((0, 0), (0, PROJ_A - split))).astype(BF16)
    wb = w_in[:, split:].astype(BF16)
    w_ukv = w['w_ukv'][l].reshape(KV_LORA, N_HEADS, QK_NOPE + V_HEAD)
    wukv_t = jnp.concatenate([
        w_ukv[:, :, :QK_NOPE].reshape(KV_LORA, N_HEADS * QK_NOPE),
        w_ukv[:, :, QK_NOPE:].reshape(KV_LORA, N_HEADS * V_HEAD)], axis=1).T.astype(BF16)
    return {
        'ff1_w_in': w['ff1_w_in'][l].astype(BF16), 'ff1_w_out': w['ff1_w_out'][l].astype(BF16),
        'ff2_w_in': w['ff2_w_in'][l].astype(BF16), 'ff2_w_out': w['ff2_w_out'][l].astype(BF16),
        'wa': wa, 'wb': wb,
        'gqa': w['q_a_norm'][l].reshape(1, Q_LORA), 'gkva': w['kv_a_norm'][l].reshape(1, KV_LORA),
        'wuq_t': w['w_uq'][l].T.astype(BF16), 'wukv_t': wukv_t,
        'gq': (w['q_norm'][l] * (ATTN_SCALE * math.log2(math.e))).reshape(QK_HEAD, 1),
        'gk': w['k_norm'][l].reshape(QK_HEAD, 1),
        'conv_w': w['conv_w'][l], 'conv_b': w['conv_b'][l].reshape(1, LRU_WIDTH),
        'wg': [_block_diag_gates(w['w_rgate'][l, d], w['w_igate'][l, d]) for d in range(2)],
        'bg': [jnp.concatenate([w['b_rgate'][l, d], w['b_igate'][l, d]]).reshape(1, 2 * LRU_WIDTH)
               for d in range(2)],
        'lam': [w['lru_lambda'][l, d].reshape(1, LRU_WIDTH) for d in range(2)],
        'w_o_attn': w['w_o_attn'][l].astype(BF16), 'w_o_lru': w['w_o_lru'][l].astype(BF16),
        'w_out': w['w_out'][l].astype(BF16),
    }


def _rope_tables_t(n_tokens):
    rows = n_tokens // GRID_W
    row = jnp.repeat(jnp.arange(rows, dtype=F32), GRID_W)
    col = jnp.tile(jnp.arange(GRID_W, dtype=F32), rows)
    inv_freq = ROPE_BASE ** (-jnp.arange(ROPE_PAIRS, dtype=F32) / ROPE_PAIRS)
    ang = jnp.concatenate([inv_freq[:, None] * row[None, :], inv_freq[:, None] * col[None, :]], axis=0)
    return jnp.cos(ang), jnp.sin(ang)


def kernel(x, c, ctx, c_ctx, w_ada, b_ada, ff1_w_in, ff1_w_out, ff2_w_in, ff2_w_out, w_in, q_a_norm, w_uq,
           kv_a_norm, w_ukv, q_norm, k_norm, conv_w, conv_b, lru_lambda, w_rgate, b_rgate, w_igate, b_igate,
           w_o_attn, w_o_lru, w_out):
    weights = dict(ff1_w_in=ff1_w_in, ff1_w_out=ff1_w_out, ff2_w_in=ff2_w_in, ff2_w_out=ff2_w_out, w_in=w_in,
                   q_a_norm=q_a_norm, w_uq=w_uq, kv_a_norm=kv_a_norm, w_ukv=w_ukv, q_norm=q_norm, k_norm=k_norm,
                   conv_w=conv_w, conv_b=conv_b, lru_lambda=lru_lambda, w_rgate=w_rgate, b_rgate=b_rgate,
                   w_igate=w_igate, b_igate=b_igate, w_o_attn=w_o_attn, w_o_lru=w_o_lru, w_out=w_out)
    batch, seq, d = x.shape
    c_len = ctx.shape[1]
    depth = w_ada.shape[0]
    assert d == D_MODEL and c_len == KV_TILE and seq % KV_TILE == 0 and batch < SUBLANES
    ctx_row = batch
    tm = 256
    tq = 512

    cond_raw = jnp.zeros((SUBLANES, d), F32).at[:batch].set(c).at[ctx_row].set(c_ctx)
    mods = _ada_call(cond_raw, w_ada, b_ada)
    mods = mods.reshape(depth, SUBLANES, N_MOD, 1, d).transpose(0, 2, 1, 3, 4)

    cos_l, sin_l = _rope_tables_t(seq)
    cos_c = jnp.ones((2 * ROPE_PAIRS, c_len), F32)
    sin_c = jnp.zeros((2 * ROPE_PAIRS, c_len), F32)

    xl, xc = x, ctx
    for l in range(depth):
        p = _layer_params(l, weights)
        m = mods[l]
        last = l == depth - 1
        xl = _ffn_call(xl, m[0], m[1], m[2], p['ff1_w_in'], p['ff1_w_out'], None, tm)
        xc = _ffn_call(xc, m[0], m[1], m[2], p['ff1_w_in'], p['ff1_w_out'], ctx_row, c_len)
        qt_l, k_l, v_l, ux_l, uy_l, gs_l = _proj_call(xl, m[3], m[4], p, cos_l, sin_l, None, tm)
        qt_c, k_c, v_c, ux_c, uy_c, gs_c = _proj_call(xc, m[3], m[4], p, cos_c, sin_c, ctx_row, c_len)
        att_l = _attn_call(qt_l, [(k_l, v_l), (k_c, v_c)], tq, unroll=1)
        hb_c, hb_l = _lru_call(ux_c, ux_l, p, 1, True)
        lru_c, lru_l = _lru_call(ux_c, ux_l, p, 0, False, combine_with=(hb_c, hb_l, uy_c, uy_l))
        xl = _merge_call(xl, att_l, lru_l, gs_l, m[5], p, None, tm)
        xl = _ffn_call(xl, m[6], m[7], m[8], p['ff2_w_in'], p['ff2_w_out'], None, tm)
        if not last:
            att_c = _attn_call(qt_c, [(k_c, v_c)], c_len)
            xc = _merge_call(xc, att_c, lru_c, gs_c, m[5], p, ctx_row, c_len)
            xc = _ffn_call(xc, m[6], m[7], m[8], p['ff2_w_in'], p['ff2_w_out'], ctx_row, c_len)
    return xl
```

```python
import functools
import math

import jax
import jax.numpy as jnp
from jax import lax
from jax.experimental import pallas as pl
from jax.experimental.pallas import tpu as pltpu

F32 = jnp.float32
BF16 = jnp.bfloat16

D_MODEL = 1024
N_HEADS = 8
QK_NOPE = 64
QK_ROPE = 32
QK_HEAD = QK_NOPE + QK_ROPE
V_HEAD = 64
Q_LORA = 384
KV_LORA = 256
LRU_WIDTH = 512
LRU_BLOCKS = 8
LRU_BLOCK = LRU_WIDTH // LRU_BLOCKS
CONV_W = 4
RG_C = 8.0
D_FF = 2816
N_MOD = 9
EPS = 1e-6
GRID_W = 64
ROPE_PAIRS = QK_ROPE // 4
ROPE_BASE = 10000.0
ATTN_SCALE = QK_HEAD ** -0.5

LANES = 128
SUBLANES = 8
MXU_DIM = 256
VMEM_LIMIT_BYTES = 60 * 1024 * 1024

HEAD_PAD = LANES
V_AUG = V_HEAD + 16
Q_BLOCK_LANES = MXU_DIM
KV_TILE = MXU_DIM
PROJ_A = 768
PROJ_B = 2 * LRU_WIDTH + 2 * D_MODEL
ADA_TN = 1152

NT_DIMS = (((1,), (1,)), ((), ()))
TN_DIMS = (((0,), (0,)), ((), ()))


def _dot(a, b):
    return jnp.dot(a, b, preferred_element_type=F32)


def _rms_rows(x):
    return x * lax.rsqrt(jnp.mean(x * x, axis=-1, keepdims=True) + EPS)


def _cparams(sem, flags=None):
    return pltpu.CompilerParams(dimension_semantics=sem, vmem_limit_bytes=VMEM_LIMIT_BYTES, flags=flags)


def _const_spec(shape):
    nd = len(shape)
    return pl.BlockSpec(shape, lambda *_: (0,) * nd, pipeline_mode=pl.Buffered(1))


def _ada_kernel(c_ref, w_ref, b_ref, o_ref):
    c = c_ref[...]
    cond = c * jax.nn.sigmoid(c)
    o_ref[0] = jnp.dot(cond, w_ref[0], preferred_element_type=F32,
                       precision=lax.Precision.HIGHEST) + b_ref[0]


def _ada_call(cond_raw, w_ada, b_ada):
    depth, d, n = w_ada.shape
    rows = cond_raw.shape[0]
    return pl.pallas_call(
        _ada_kernel,
        grid=(depth, n // ADA_TN),
        in_specs=[
            pl.BlockSpec((rows, d), lambda l, j: (0, 0)),
            pl.BlockSpec((1, d, ADA_TN), lambda l, j: (l, 0, j)),
            pl.BlockSpec((1, 1, ADA_TN), lambda l, j: (l, 0, j)),
        ],
        out_specs=pl.BlockSpec((1, rows, ADA_TN), lambda l, j: (l, 0, j)),
        out_shape=jax.ShapeDtypeStruct((depth, rows, n), F32),
        compiler_params=_cparams(("arbitrary", "arbitrary")),
        name="ada",
    )(cond_raw, w_ada, b_ada.reshape(depth, 1, n))


def _mod_spec(mod_row):
    if mod_row is None:
        return pl.BlockSpec((1, 1, D_MODEL), lambda b, i: (b, 0, 0))
    return pl.BlockSpec((1, 1, D_MODEL), lambda b, i: (mod_row, 0, 0))


def _ffn_kernel(x_ref, sh_ref, sc_ref, g_ref, win_ref, wout_ref, o_ref):
    x = x_ref[0]
    xb = (_rms_rows(x) * (1.0 + sc_ref[0]) + sh_ref[0]).astype(BF16)
    g = _dot(xb, win_ref[:, :D_FF])
    u = _dot(xb, win_ref[:, D_FF:])
    a = (g * jax.nn.sigmoid(g) * u).astype(BF16)
    y = _dot(a, wout_ref[...])
    o_ref[0] = x + (0.5 * g_ref[0]) * y


def _ffn_call(x, sh, sc, g, w_in, w_out, mod_row, tm):
    b, t, d = x.shape
    return pl.pallas_call(
        _ffn_kernel,
        grid=(b, t // tm),
        in_specs=[
            pl.BlockSpec((1, tm, d), lambda bi, i: (bi, i, 0)),
            _mod_spec(mod_row), _mod_spec(mod_row), _mod_spec(mod_row),
            _const_spec(w_in.shape), _const_spec(w_out.shape),
        ],
        out_specs=pl.BlockSpec((1, tm, d), lambda bi, i: (bi, i, 0)),
        out_shape=jax.ShapeDtypeStruct(x.shape, F32),
        compiler_params=_cparams(("arbitrary", "arbitrary")),
        name="ffn",
    )(x, sh, sc, g, w_in, w_out)


def _rope_t(t, cos, sin):
    p = ROPE_PAIRS
    x1r, x2r, x1c, x2c = t[0:p], t[p:2 * p], t[2 * p:3 * p], t[3 * p:4 * p]
    cr, cc = cos[0:p], cos[p:2 * p]
    sr, sc = sin[0:p], sin[p:2 * p]
    return jnp.concatenate(
        [x1r * cr - x2r * sr, x1r * sr + x2r * cr, x1c * cc - x2c * sc, x1c * sc + x2c * cc], axis=0)


def _proj_kernel(x_ref, sh_ref, sc_ref, wa_ref, wb_ref, gqa_ref, gkva_ref, wuq_ref, wukv_ref,
                 gq_ref, gk_ref, cos_ref, sin_ref,
                 qt_ref, k_ref, v_ref, ux_ref, uy_ref, gs_ref):
    tm = x_ref.shape[1]
    x = x_ref[0]
    xb = (_rms_rows(x) * (1.0 + sc_ref[0]) + sh_ref[0]).astype(BF16)
    ha = _dot(xb, wa_ref[...])
    hb = _dot(xb, wb_ref[...])
    ux_ref[0] = hb[:, :LRU_WIDTH]
    uy_ref[0] = hb[:, LRU_WIDTH:2 * LRU_WIDTH]
    gs_ref[0] = jax.nn.sigmoid(hb[:, 2 * LRU_WIDTH:]).astype(BF16)

    qn = (_rms_rows(ha[:, :Q_LORA]) * gqa_ref[...]).astype(BF16)
    kvn = (_rms_rows(ha[:, Q_LORA:Q_LORA + KV_LORA]) * gkva_ref[...]).astype(BF16)
    q_t = lax.dot_general(wuq_ref[...], qn, NT_DIMS, preferred_element_type=F32)
    kv_t = lax.dot_general(wukv_ref[...], kvn, NT_DIMS, preferred_element_type=F32)
    kr_t = ha[:, Q_LORA + KV_LORA:].T[:QK_ROPE]

    cos = cos_ref[...]
    sin = sin_ref[...]
    gq = gq_ref[...]
    gk = gk_ref[...]
    kr_rot = _rope_t(kr_t * gk[QK_NOPE:], cos, sin)
    kr_ss = jnp.sum(kr_t * kr_t, axis=0, keepdims=True)
    zpad = jnp.zeros((HEAD_PAD - QK_HEAD, tm), F32)
    ones_rows = jnp.where(lax.broadcasted_iota(jnp.int32, (V_AUG - V_HEAD, tm), 0) == 0, 1.0, 0.0)
    inv_d = 1.0 / QK_HEAD
    for h in range(N_HEADS):
        q = q_t[QK_HEAD * h:QK_HEAD * (h + 1)]
        rq = lax.rsqrt(jnp.sum(q * q, axis=0, keepdims=True) * inv_d + EPS)
        qs = q * rq * gq
        qh = jnp.concatenate([qs[:QK_NOPE], _rope_t(qs[QK_NOPE:], cos, sin), zpad], axis=0)
        qt_ref[0, HEAD_PAD * h:HEAD_PAD * (h + 1), :] = qh.astype(BF16)

        kn = kv_t[QK_NOPE * h:QK_NOPE * (h + 1)]
        rk = lax.rsqrt((jnp.sum(kn * kn, axis=0, keepdims=True) + kr_ss) * inv_d + EPS)
        kh = jnp.concatenate([kn * gk[:QK_NOPE] * rk, kr_rot * rk, zpad], axis=0)
        k_ref[0, h] = kh.T.astype(BF16)

        v0 = N_HEADS * QK_NOPE + V_HEAD * h
        vt = jnp.concatenate([kv_t[v0:v0 + V_HEAD], ones_rows], axis=0).astype(BF16)
        for c in range(tm // KV_TILE):
            v_ref[0, h, c] = vt[:, KV_TILE * c:KV_TILE * (c + 1)]


def _proj_call(x, sh, sc, p, cos_t, sin_t, mod_row, tm):
    b, t, d = x.shape
    nkv = t // KV_TILE
    out_shape = (
        jax.ShapeDtypeStruct((b, N_HEADS * HEAD_PAD, t), BF16),
        jax.ShapeDtypeStruct((b, N_HEADS, t, HEAD_PAD), BF16),
        jax.ShapeDtypeStruct((b, N_HEADS, nkv, V_AUG, KV_TILE), BF16),
        jax.ShapeDtypeStruct((b, t, LRU_WIDTH), F32),
        jax.ShapeDtypeStruct((b, t, LRU_WIDTH), F32),
        jax.ShapeDtypeStruct((b, t, 2 * D_MODEL), BF16),
    )
    out_specs = (
        pl.BlockSpec((1, N_HEADS * HEAD_PAD, tm), lambda bi, i: (bi, 0, i)),
        pl.BlockSpec((1, N_HEADS, tm, HEAD_PAD), lambda bi, i: (bi, 0, i, 0)),
        pl.BlockSpec((1, N_HEADS, tm // KV_TILE, V_AUG, KV_TILE), lambda bi, i: (bi, 0, i, 0, 0)),
        pl.BlockSpec((1, tm, LRU_WIDTH), lambda bi, i: (bi, i, 0)),
        pl.BlockSpec((1, tm, LRU_WIDTH), lambda bi, i: (bi, i, 0)),
        pl.BlockSpec((1, tm, 2 * D_MODEL), lambda bi, i: (bi, i, 0)),
    )
    consts = (p['wa'], p['wb'], p['gqa'], p['gkva'], p['wuq_t'], p['wukv_t'], p['gq'], p['gk'])
    return pl.pallas_call(
        _proj_kernel,
        grid=(b, t // tm),
        in_specs=[
            pl.BlockSpec((1, tm, d), lambda bi, i: (bi, i, 0)),
            _mod_spec(mod_row), _mod_spec(mod_row),
            *[_const_spec(c.shape) for c in consts],
            pl.BlockSpec((2 * ROPE_PAIRS, tm), lambda bi, i: (0, i)),
            pl.BlockSpec((2 * ROPE_PAIRS, tm), lambda bi, i: (0, i)),
        ],
        out_specs=out_specs,
        out_shape=out_shape,
        compiler_params=_cparams(("arbitrary", "arbitrary")),
        name="proj",
    )(x, sh, sc, *consts, cos_t, sin_t)


def _col_max(s):
    rows = s.shape[0]
    while rows > 4 * SUBLANES and rows % 2 == 0:
        rows //= 2
        s = jnp.maximum(s[:rows], s[rows:])
    return jnp.max(s, axis=0, keepdims=True)


def _attn_kernel(*refs, n_tiles, unroll):
    n_kv = 2 * len(n_tiles)
    q_ref = refs[0]
    kv_refs = refs[1:1 + n_kv]
    o_ref = refs[1 + n_kv]
    s_scr, mt_scr = refs[2 + n_kv:]
    tq = q_ref.shape[2]
    n_blocks = tq // Q_BLOCK_LANES
    q_blocks = [q_ref[0, :, Q_BLOCK_LANES * i:Q_BLOCK_LANES * (i + 1)] for i in range(n_blocks)]

    def scores(kt, slot):
        for b, q_b in enumerate(q_blocks):
            s = _dot(kt, q_b)
            s_scr[slot, b] = s
            mt_scr[slot, b] = _col_max(s)

    def update(slot, vt, state):
        out = []
        for b, (m, acc) in enumerate(state):
            m_new = jnp.maximum(m, mt_scr[slot, b])
            p = jnp.exp2(s_scr[slot, b] - m_new).astype(BF16)
            alpha = jnp.exp2(m - m_new)
            out.append((m_new, alpha * acc + _dot(vt, p)))
        return tuple(out)

    tiles = []
    for src, n in enumerate(n_tiles):
        k_ref, v_ref = kv_refs[2 * src], kv_refs[2 * src + 1]
        tiles += [(k_ref, v_ref, j) for j in range(n)]
    k0_ref, v0_ref, n0 = kv_refs[0], kv_refs[1], n_tiles[0]
    n_pairs = (n0 - 2) // 2 if len(tiles) > n0 else 0
    n_loop = 2 * n_pairs

    def k_tile(k_ref, j):
        off = j * KV_TILE if isinstance(j, int) else pl.multiple_of(j * KV_TILE, KV_TILE)
        return k_ref[0, 0, pl.ds(off, KV_TILE), :]

    state = tuple((jnp.full((1, Q_BLOCK_LANES), -jnp.inf, F32), jnp.zeros((V_AUG, Q_BLOCK_LANES), F32))
                  for _ in range(n_blocks))
    scores(k_tile(tiles[0][0], 0), 0)
    if n_pairs:
        def body(g, st):
            j = 2 * g
            scores(k_tile(k0_ref, j + 1), 1)
            st = update(0, v0_ref[0, 0, j], st)
            scores(k_tile(k0_ref, j + 2), 0)
            return update(1, v0_ref[0, 0, j + 1], st)
        state = lax.fori_loop(0, n_pairs, body, state, unroll=unroll)
    for i in range(n_loop, len(tiles)):
        slot = (i - n_loop) % 2
        if i + 1 < len(tiles):
            scores(k_tile(tiles[i + 1][0], tiles[i + 1][2]), 1 - slot)
        state = update(slot, tiles[i][1][0, 0, tiles[i][2]], state)
    for i, (_, acc) in enumerate(state):
        o_ref[0, :, Q_BLOCK_LANES * i:Q_BLOCK_LANES * (i + 1)] = (
            acc[:V_HEAD] / acc[V_HEAD:V_HEAD + 1]).astype(BF16)


def _attn_call(q_t, kv_sources, tq, unroll=1):
    b, hp, t = q_t.shape
    in_specs = [pl.BlockSpec((1, HEAD_PAD, tq), lambda bi, h, i: (bi, h, i))]
    args = [q_t]
    n_tiles = []
    for k, v in kv_sources:
        tk = k.shape[2]
        n_tiles.append(tk // KV_TILE)
        in_specs.append(pl.BlockSpec((1, 1, tk, HEAD_PAD), lambda bi, h, i: (bi, h, 0, 0)))
        in_specs.append(pl.BlockSpec((1, 1, tk // KV_TILE, V_AUG, KV_TILE), lambda bi, h, i: (bi, h, 0, 0, 0)))
        args += [k, v]
    return pl.pallas_call(
        functools.partial(_attn_kernel, n_tiles=tuple(n_tiles), unroll=unroll),
        grid=(b, N_HEADS, t // tq),
        in_specs=in_specs,
        out_specs=pl.BlockSpec((1, V_HEAD, tq), lambda bi, h, i: (bi, h, i)),
        out_shape=jax.ShapeDtypeStruct((b, N_HEADS * V_HEAD, t), BF16),
        scratch_shapes=[
            pltpu.VMEM((2, tq // Q_BLOCK_LANES, KV_TILE, Q_BLOCK_LANES), F32),
            pltpu.VMEM((2, tq // Q_BLOCK_LANES, 1, Q_BLOCK_LANES), F32),
        ],
        compiler_params=_cparams(("arbitrary", "arbitrary", "arbitrary")),
        name="attn",
    )(*args)


def _gelu_tanh(x):
    return 0.5 * x * (1.0 + jnp.tanh(math.sqrt(2.0 / math.pi) * (x + 0.044715 * (x * x * x))))


def _lru_kernel(*refs, reverse, n_chunks, chunk, combine):
    (uxc_ref, ux_ref, uxp_ref, uxn_ref, cw_ref, cb_ref, wg_ref, bg_ref, lam_ref) = refs[:9]
    if combine:
        hbc_ref, hbl_ref, uyc_ref, uy_ref = refs[9:13]
        outc_ref, outl_ref, xbuf, a_s, b_s, h_s, carry_s = refs[13:]
    else:
        outc_ref, outl_ref, xbuf, a_s, b_s, h_s, carry_s = refs[9:]
    width = LRU_WIDTH
    halo = SUBLANES
    s = pl.program_id(1)
    is_ctx = s == 0
    c = jnp.clip((n_chunks - s) if reverse else (s - 1), 0, n_chunks - 1)

    zero_h = jnp.zeros((halo, width), F32)
    xbuf[0:halo] = jnp.where(is_ctx | (c == 0), zero_h, uxp_ref[0])
    xbuf[halo:halo + chunk] = jnp.where(is_ctx, uxc_ref[0], ux_ref[0])
    xbuf[halo + chunk:2 * halo + chunk] = jnp.where(is_ctx | (c == n_chunks - 1), zero_h, uxn_ref[0])
    cw = cw_ref[...]
    xc = cb_ref[...]
    for k in range(CONV_W):
        o = halo - 2 + k
        xc = xc + cw[k:k + 1] * xbuf[o:o + chunk]

    xcb = xc.astype(BF16)
    half = width // 2
    d0 = _dot(xcb[:, :half], wg_ref[0])
    d1 = _dot(xcb[:, half:], wg_ref[1])
    bg = bg_ref[...]
    r = jax.nn.sigmoid(jnp.concatenate([d0[:, :half], d1[:, :half]], axis=1) + bg[:, :width])
    i = jax.nn.sigmoid(jnp.concatenate([d0[:, half:], d1[:, half:]], axis=1) + bg[:, width:])
    nl = -lam_ref[...]
    softplus = jnp.maximum(nl, 0.0) + jnp.log(1.0 + jnp.exp(-jnp.abs(nl)))
    a = jnp.exp(-RG_C * r * softplus)
    a_s[...] = a
    b_s[...] = jnp.sqrt(1.0 - a * a) * (i * xc)

    row = lax.broadcasted_iota(jnp.int32, (SUBLANES, width), 0)
    n_groups = chunk // SUBLANES

    def body(g, carry):
        gg = (n_groups - 1 - g) if reverse else g
        off = pl.multiple_of(gg * SUBLANES, SUBLANES)
        av = a_s[pl.ds(off, SUBLANES), :]
        bv = b_s[pl.ds(off, SUBLANES), :]
        for k in (1, 2, 4):
            shift = (SUBLANES - k) if reverse else k
            valid = (row < SUBLANES - k) if reverse else (row >= k)
            a_sh = jnp.where(valid, pltpu.roll(av, shift, 0), 1.0)
            b_sh = jnp.where(valid, pltpu.roll(bv, shift, 0), 0.0)
            bv = av * b_sh + bv
            av = av * a_sh
        hblk = av * carry + bv
        h_s[pl.ds(off, SUBLANES), :] = hblk
        return hblk[0:1] if reverse else hblk[SUBLANES - 1:SUBLANES]

    carry0 = jnp.where(is_ctx, jnp.zeros((1, width), F32), carry_s[...])
    carry_s[...] = lax.fori_loop(0, n_groups, body, carry0)
    h = h_s[...]

    if combine:
        hb = jnp.where(is_ctx, hbc_ref[0], hbl_ref[0])
        uy = jnp.where(is_ctx, uyc_ref[0], uy_ref[0])
        val = ((h + hb) * _gelu_tanh(uy)).astype(BF16)
    else:
        val = h

    @pl.when(is_ctx)
    def _():
        outc_ref[0] = val

    @pl.when(jnp.logical_not(is_ctx))
    def _():
        outl_ref[0] = val


def _lru_call(ux_c, ux_l, p, direction, reverse, combine_with=None):
    b, s_len, w = ux_l.shape
    c_len = ux_c.shape[1]
    chunk = c_len
    assert s_len % chunk == 0 and chunk % SUBLANES == 0
    n = s_len // chunk
    hpc = chunk // SUBLANES
    n_hblk = s_len // SUBLANES

    def cidx(s):
        return jnp.clip((n - s) if reverse else (s - 1), 0, n - 1)

    ctx_spec = pl.BlockSpec((1, c_len, w), lambda bi, s: (bi, 0, 0))
    lat_spec = pl.BlockSpec((1, chunk, w), lambda bi, s: (bi, cidx(s), 0))
    in_specs = [
        ctx_spec, lat_spec,
        pl.BlockSpec((1, SUBLANES, w), lambda bi, s: (bi, jnp.maximum(cidx(s) * hpc - 1, 0), 0)),
        pl.BlockSpec((1, SUBLANES, w), lambda bi, s: (bi, jnp.minimum((cidx(s) + 1) * hpc, n_hblk - 1), 0)),
    ]
    consts = (p['conv_w'], p['conv_b'], p['wg'][direction], p['bg'][direction], p['lam'][direction])
    in_specs += [_const_spec(cst.shape) for cst in consts]
    args = [ux_c, ux_l, ux_l, ux_l, *consts]
    combine = combine_with is not None
    if combine:
        in_specs += [ctx_spec, lat_spec, ctx_spec, lat_spec]
        args += list(combine_with)
    out_dtype = BF16 if combine else F32
    scratch = [
        pltpu.VMEM((chunk + 2 * SUBLANES, w), F32),
        pltpu.VMEM((chunk, w), F32), pltpu.VMEM((chunk, w), F32), pltpu.VMEM((chunk, w), F32),
        pltpu.VMEM((1, w), F32),
    ]
    return pl.pallas_call(
        functools.partial(_lru_kernel, reverse=reverse, n_chunks=n, chunk=chunk, combine=combine),
        grid=(b, n + 1),
        in_specs=in_specs,
        out_specs=(ctx_spec, lat_spec),
        out_shape=(jax.ShapeDtypeStruct(ux_c.shape, out_dtype), jax.ShapeDtypeStruct(ux_l.shape, out_dtype)),
        scratch_shapes=scratch,
        compiler_params=_cparams(("arbitrary", "arbitrary")),
        name="lru_fwd" if combine else "lru_bwd",
    )(*args)


def _merge_kernel(x_ref, att_ref, lru_ref, gs_ref, g_ref, woa_ref, wol_ref, wout_ref, o_ref):
    att = lax.dot_general(att_ref[0], woa_ref[...], TN_DIMS, preferred_element_type=F32)
    lru = _dot(lru_ref[0], wol_ref[...])
    gs = gs_ref[0].astype(F32)
    merged = (gs[:, :D_MODEL] * att + gs[:, D_MODEL:] * lru).astype(BF16)
    o_ref[0] = x_ref[0] + g_ref[0] * _dot(merged, wout_ref[...])


def _merge_call(x, att_t, lru, gs, g, p, mod_row, tm):
    b, t, d = x.shape
    consts = (p['w_o_attn'], p['w_o_lru'], p['w_out'])
    return pl.pallas_call(
        _merge_kernel,
        grid=(b, t // tm),
        in_specs=[
            pl.BlockSpec((1, tm, d), lambda bi, i: (bi, i, 0)),
            pl.BlockSpec((1, N_HEADS * V_HEAD, tm), lambda bi, i: (bi, 0, i)),
            pl.BlockSpec((1, tm, LRU_WIDTH), lambda bi, i: (bi, i, 0)),
            pl.BlockSpec((1, tm, 2 * D_MODEL), lambda bi, i: (bi, i, 0)),
            _mod_spec(mod_row),
            *[_const_spec(c.shape) for c in consts],
        ],
        out_specs=pl.BlockSpec((1, tm, d), lambda bi, i: (bi, i, 0)),
        out_shape=jax.ShapeDtypeStruct(x.shape, F32),
        compiler_params=_cparams(("arbitrary", "arbitrary")),
        name="merge",
    )(x, att_t, lru, gs, g, *consts)


def _block_diag_gates(w_r, w_i):
    def bd(w4):
        z = jnp.zeros((4, LRU_BLOCK, 4, LRU_BLOCK), w4.dtype)
        idx = jnp.arange(4)
        z = z.at[idx, :, idx, :].set(w4)
        return z.reshape(4 * LRU_BLOCK, 4 * LRU_BLOCK)
    groups = []
    for j in range(2):
        groups.append(jnp.concatenate([bd(w_r[4 * j:4 * j + 4]), bd(w_i[4 * j:4 * j + 4])], axis=1))
    return jnp.stack(groups).astype(BF16)


def _layer_params(l, w):
    w_in = w['w_in'][l]
    split = Q_LORA + KV_LORA + QK_ROPE
    wa = jnp.pad(w_in[:, :split], ((0, 0), (0, PROJ_A - split))).astype(BF16)
    wb = w_in[:, split:].astype(BF16)
    w_ukv = w['w_ukv'][l].reshape(KV_LORA, N_HEADS, QK_NOPE + V_HEAD)
    wukv_t = jnp.concatenate([
        w_ukv[:, :, :QK_NOPE].reshape(KV_LORA, N_HEADS * QK_NOPE),
        w_ukv[:, :, QK_NOPE:].reshape(KV_LORA, N_HEADS * V_HEAD)], axis=1).T.astype(BF16)
    return {
        'ff1_w_in': w['ff1_w_in'][l].astype(BF16), 'ff1_w_out': w['ff1_w_out'][l].astype(BF16),
        'ff2_w_in': w['ff2_w_in'][l].astype(BF16), 'ff2_w_out': w['ff2_w_out'][l].astype(BF16),
        'wa': wa, 'wb': wb,
        'gqa': w['q_a_norm'][l].reshape(1, Q_LORA), 'gkva': w['kv_a_norm'][l].reshape(1, KV_LORA),
        'wuq_t': w['w_uq'][l].T.astype(BF16), 'wukv_t': wukv_t,
        'gq': (w['q_norm'][l] * (ATTN_SCALE * math.log2(math.e))).reshape(QK_HEAD, 1),
        'gk': w['k_norm'][l].reshape(QK_HEAD, 1),
        'conv_w': w['conv_w'][l], 'conv_b': w['conv_b'][l].reshape(1, LRU_WIDTH),
        'wg': [_block_diag_gates(w['w_rgate'][l, d], w['w_igate'][l, d]) for d in range(2)],
        'bg': [jnp.concatenate([w['b_rgate'][l, d], w['b_igate'][l, d]]).reshape(1, 2 * LRU_WIDTH)
               for d in range(2)],
        'lam': [w['lru_lambda'][l, d].reshape(1, LRU_WIDTH) for d in range(2)],
        'w_o_attn': w['w_o_attn'][l].astype(BF16), 'w_o_lru': w['w_o_lru'][l].astype(BF16),
        'w_out': w['w_out'][l].astype(BF16),
    }


def _rope_tables_t(n_tokens):
    rows = n_tokens // GRID_W
    row = jnp.repeat(jnp.arange(rows, dtype=F32), GRID_W)
    col = jnp.tile(jnp.arange(GRID_W, dtype=F32), rows)
    inv_freq = ROPE_BASE ** (-jnp.arange(ROPE_PAIRS, dtype=F32) / ROPE_PAIRS)
    ang = jnp.concatenate([inv_freq[:, None] * row[None, :], inv_freq[:, None] * col[None, :]], axis=0)
    return jnp.cos(ang), jnp.sin(ang)


def kernel(x, c, ctx, c_ctx, w_ada, b_ada, ff1_w_in, ff1_w_out, ff2_w_in, ff2_w_out, w_in, q_a_norm, w_uq,
           kv_a_norm, w_ukv, q_norm, k_norm, conv_w, conv_b, lru_lambda, w_rgate, b_rgate, w_igate, b_igate,
           w_o_attn, w_o_lru, w_out):
    weights = dict(ff1_w_in=ff1_w_in, ff1_w_out=ff1_w_out, ff2_w_in=ff2_w_in, ff2_w_out=ff2_w_out, w_in=w_in,
                   q_a_norm=q_a_norm, w_uq=w_uq, kv_a_norm=kv_a_norm, w_ukv=w_ukv, q_norm=q_norm, k_norm=k_norm,
                   conv_w=conv_w, conv_b=conv_b, lru_lambda=lru_lambda, w_rgate=w_rgate, b_rgate=b_rgate,
                   w_igate=w_igate, b_igate=b_igate, w_o_attn=w_o_attn, w_o_lru=w_o_lru, w_out=w_out)
    batch, seq, d = x.shape
    c_len = ctx.shape[1]
    depth = w_ada.shape[0]
    assert d == D_MODEL and c_len == KV_TILE and seq % KV_TILE == 0 and batch < SUBLANES
    ctx_row = batch
    tm = 256
    tq = 1024

    cond_raw = jnp.zeros((SUBLANES, d), F32).at[:batch].set(c).at[ctx_row].set(c_ctx)
    mods = _ada_call(cond_raw, w_ada, b_ada)
    mods = mods.reshape(depth, SUBLANES, N_MOD, 1, d).transpose(0, 2, 1, 3, 4)

    cos_l, sin_l = _rope_tables_t(seq)
    cos_c = jnp.ones((2 * ROPE_PAIRS, c_len), F32)
    sin_c = jnp.zeros((2 * ROPE_PAIRS, c_len), F32)

    xl, xc = x, ctx
    for l in range(depth):
        p = _layer_params(l, weights)
        m = mods[l]
        last = l == depth - 1
        xl = _ffn_call(xl, m[0], m[1], m[2], p['ff1_w_in'], p['ff1_w_out'], None, tm)
        xc = _ffn_call(xc, m[0], m[1], m[2], p['ff1_w_in'], p['ff1_w_out'], ctx_row, c_len)
        qt_l, k_l, v_l, ux_l, uy_l, gs_l = _proj_call(xl, m[3], m[4], p, cos_l, sin_l, None, tm)
        qt_c, k_c, v_c, ux_c, uy_c, gs_c = _proj_call(xc, m[3], m[4], p, cos_c, sin_c, ctx_row, c_len)
        att_l = _attn_call(qt_l, [(k_l, v_l), (k_c, v_c)], tq, unroll=3)
        hb_c, hb_l = _lru_call(ux_c, ux_l, p, 1, True)
        lru_c, lru_l = _lru_call(ux_c, ux_l, p, 0, False, combine_with=(hb_c, hb_l, uy_c, uy_l))
        xl = _merge_call(xl, att_l, lru_l, gs_l, m[5], p, None, tm)
        xl = _ffn_call(xl, m[6], m[7], m[8], p['ff2_w_in'], p['ff2_w_out'], None, tm)
        if not last:
            att_c = _attn_call(qt_c, [(k_c, v_c)], c_len)
            xc = _merge_call(xc, att_c, lru_c, gs_c, m[5], p, ctx_row, c_len)
            xc = _ffn_call(xc, m[6], m[7], m[8], p['ff2_w_in'], p['ff2_w_out'], ctx_row, c_len)
    return xl
```

```python
import functools
import math

import jax
import jax.numpy as jnp
from jax import lax
from jax.experimental import pallas as pl
from jax.experimental.pallas import tpu as pltpu

F32 = jnp.float32
BF16 = jnp.bfloat16

D_MODEL = 1024
N_HEADS = 8
QK_NOPE = 64
QK_ROPE = 32
QK_HEAD = QK_NOPE + QK_ROPE
V_HEAD = 64
Q_LORA = 384
KV_LORA = 256
LRU_WIDTH = 512
LRU_BLOCKS = 8
LRU_BLOCK = LRU_WIDTH // LRU_BLOCKS
CONV_W = 4
RG_C = 8.0
D_FF = 2816
N_MOD = 9
EPS = 1e-6
GRID_W = 64
ROPE_PAIRS = QK_ROPE // 4
ROPE_BASE = 10000.0
ATTN_SCALE = QK_HEAD ** -0.5

LANES = 128
SUBLANES = 8
MXU_DIM = 256
VMEM_LIMIT_BYTES = 60 * 1024 * 1024

HEAD_PAD = LANES
V_AUG = V_HEAD + 16
Q_BLOCK_LANES = MXU_DIM
KV_TILE = MXU_DIM
PROJ_A = 768
PROJ_B = 2 * LRU_WIDTH + 2 * D_MODEL
ADA_TN = 1152
ATTN_GROUP = 1
ATTN_PAIR_UNROLL = 4

NT_DIMS = (((1,), (1,)), ((), ()))
TN_DIMS = (((0,), (0,)), ((), ()))


def _dot(a, b):
    return jnp.dot(a, b, preferred_element_type=F32)


def _rms_rows(x):
    return x * lax.rsqrt(jnp.mean(x * x, axis=-1, keepdims=True) + EPS)


def _cparams(sem, flags=None):
    return pltpu.CompilerParams(dimension_semantics=sem, vmem_limit_bytes=VMEM_LIMIT_BYTES, flags=flags)


def _const_spec(shape):
    nd = len(shape)
    return pl.BlockSpec(shape, lambda *_: (0,) * nd, pipeline_mode=pl.Buffered(1))


def _ada_kernel(c_ref, w_ref, b_ref, o_ref):
    c = c_ref[...]
    cond = c * jax.nn.sigmoid(c)
    o_ref[0] = jnp.dot(cond, w_ref[0], preferred_element_type=F32,
                       precision=lax.Precision.HIGHEST) + b_ref[0]


def _ada_call(cond_raw, w_ada, b_ada):
    depth, d, n = w_ada.shape
    rows = cond_raw.shape[0]
    return pl.pallas_call(
        _ada_kernel,
        grid=(depth, n // ADA_TN),
        in_specs=[
            pl.BlockSpec((rows, d), lambda l, j: (0, 0)),
            pl.BlockSpec((1, d, ADA_TN), lambda l, j: (l, 0, j)),
            pl.BlockSpec((1, 1, ADA_TN), lambda l, j: (l, 0, j)),
        ],
        out_specs=pl.BlockSpec((1, rows, ADA_TN), lambda l, j: (l, 0, j)),
        out_shape=jax.ShapeDtypeStruct((depth, rows, n), F32),
        compiler_params=_cparams(("arbitrary", "arbitrary")),
        name="ada",
    )(cond_raw, w_ada, b_ada.reshape(depth, 1, n))


def _mod_spec(mod_row):
    if mod_row is None:
        return pl.BlockSpec((1, 1, D_MODEL), lambda b, i: (b, 0, 0))
    return pl.BlockSpec((1, 1, D_MODEL), lambda b, i: (mod_row, 0, 0))


def _ffn_kernel(x_ref, sh_ref, sc_ref, g_ref, win_ref, wout_ref, o_ref):
    x = x_ref[0]
    xb = (_rms_rows(x) * (1.0 + sc_ref[0]) + sh_ref[0]).astype(BF16)
    g = _dot(xb, win_ref[:, :D_FF])
    u = _dot(xb, win_ref[:, D_FF:])
    a = (g * jax.nn.sigmoid(g) * u).astype(BF16)
    y = _dot(a, wout_ref[...])
    o_ref[0] = x + (0.5 * g_ref[0]) * y


def _ffn_call(x, sh, sc, g, w_in, w_out, mod_row, tm):
    b, t, d = x.shape
    return pl.pallas_call(
        _ffn_kernel,
        grid=(b, t // tm),
        in_specs=[
            pl.BlockSpec((1, tm, d), lambda bi, i: (bi, i, 0)),
            _mod_spec(mod_row), _mod_spec(mod_row), _mod_spec(mod_row),
            _const_spec(w_in.shape), _const_spec(w_out.shape),
        ],
        out_specs=pl.BlockSpec((1, tm, d), lambda bi, i: (bi, i, 0)),
        out_shape=jax.ShapeDtypeStruct(x.shape, F32),
        compiler_params=_cparams(("arbitrary", "arbitrary")),
        name="ffn",
    )(x, sh, sc, g, w_in, w_out)


def _rope_t(t, cos, sin):
    p = ROPE_PAIRS
    x1r, x2r, x1c, x2c = t[0:p], t[p:2 * p], t[2 * p:3 * p], t[3 * p:4 * p]
    cr, cc = cos[0:p], cos[p:2 * p]
    sr, sc = sin[0:p], sin[p:2 * p]
    return jnp.concatenate(
        [x1r * cr - x2r * sr, x1r * sr + x2r * cr, x1c * cc - x2c * sc, x1c * sc + x2c * cc], axis=0)


def _proj_kernel(x_ref, sh_ref, sc_ref, wa_ref, wb_ref, gqa_ref, gkva_ref, wuq_ref, wukv_ref,
                 gq_ref, gk_ref, cos_ref, sin_ref,
                 qt_ref, k_ref, v_ref, ux_ref, uy_ref, gs_ref):
    tm = x_ref.shape[1]
    x = x_ref[0]
    xb = (_rms_rows(x) * (1.0 + sc_ref[0]) + sh_ref[0]).astype(BF16)
    ha = _dot(xb, wa_ref[...])
    hb = _dot(xb, wb_ref[...])
    ux_ref[0] = hb[:, :LRU_WIDTH]
    uy_ref[0] = hb[:, LRU_WIDTH:2 * LRU_WIDTH]
    gs_ref[0] = jax.nn.sigmoid(hb[:, 2 * LRU_WIDTH:]).astype(BF16)

    qn = (_rms_rows(ha[:, :Q_LORA]) * gqa_ref[...]).astype(BF16)
    kvn = (_rms_rows(ha[:, Q_LORA:Q_LORA + KV_LORA]) * gkva_ref[...]).astype(BF16)
    q_t = lax.dot_general(wuq_ref[...], qn, NT_DIMS, preferred_element_type=F32)
    kv_t = lax.dot_general(wukv_ref[...], kvn, NT_DIMS, preferred_element_type=F32)
    kr_t = ha[:, Q_LORA + KV_LORA:].T[:QK_ROPE]

    cos = cos_ref[...]
    sin = sin_ref[...]
    gq = gq_ref[...]
    gk = gk_ref[...]
    kr_rot = _rope_t(kr_t * gk[QK_NOPE:], cos, sin)
    kr_ss = jnp.sum(kr_t * kr_t, axis=0, keepdims=True)
    zpad = jnp.zeros((HEAD_PAD - QK_HEAD, tm), F32)
    ones_rows = jnp.where(lax.broadcasted_iota(jnp.int32, (V_AUG - V_HEAD, tm), 0) == 0, 1.0, 0.0)
    inv_d = 1.0 / QK_HEAD
    for h in range(N_HEADS):
        q = q_t[QK_HEAD * h:QK_HEAD * (h + 1)]
        rq = lax.rsqrt(jnp.sum(q * q, axis=0, keepdims=True) * inv_d + EPS)
        qs = q * rq * gq
        qh = jnp.concatenate([qs[:QK_NOPE], _rope_t(qs[QK_NOPE:], cos, sin), zpad], axis=0)
        qt_ref[0, HEAD_PAD * h:HEAD_PAD * (h + 1), :] = qh.astype(BF16)

        kn = kv_t[QK_NOPE * h:QK_NOPE * (h + 1)]
        rk = lax.rsqrt((jnp.sum(kn * kn, axis=0, keepdims=True) + kr_ss) * inv_d + EPS)
        kh = jnp.concatenate([kn * gk[:QK_NOPE] * rk, kr_rot * rk, zpad], axis=0)
        k_ref[0, h] = kh.T.astype(BF16)

        v0 = N_HEADS * QK_NOPE + V_HEAD * h
        vt = jnp.concatenate([kv_t[v0:v0 + V_HEAD], ones_rows], axis=0).astype(BF16)
        for c in range(tm // KV_TILE):
            v_ref[0, h, c] = vt[:, KV_TILE * c:KV_TILE * (c + 1)]


def _proj_call(x, sh, sc, p, cos_t, sin_t, mod_row, tm, kv_tokens, kv_offset, kv_buffers=None):
    b, t, d = x.shape
    off_blk = kv_offset // tm
    out_shape = (
        jax.ShapeDtypeStruct((b, N_HEADS * HEAD_PAD, t), BF16),
        jax.ShapeDtypeStruct((b, N_HEADS, kv_tokens, HEAD_PAD), BF16),
        jax.ShapeDtypeStruct((b, N_HEADS, kv_tokens // KV_TILE, V_AUG, KV_TILE), BF16),
        jax.ShapeDtypeStruct((b, t, LRU_WIDTH), F32),
        jax.ShapeDtypeStruct((b, t, LRU_WIDTH), F32),
        jax.ShapeDtypeStruct((b, t, 2 * D_MODEL), BF16),
    )
    out_specs = (
        pl.BlockSpec((1, N_HEADS * HEAD_PAD, tm), lambda bi, i: (bi, 0, i)),
        pl.BlockSpec((1, N_HEADS, tm, HEAD_PAD), lambda bi, i: (bi, 0, off_blk + i, 0)),
        pl.BlockSpec((1, N_HEADS, tm // KV_TILE, V_AUG, KV_TILE), lambda bi, i: (bi, 0, off_blk + i, 0, 0)),
        pl.BlockSpec((1, tm, LRU_WIDTH), lambda bi, i: (bi, i, 0)),
        pl.BlockSpec((1, tm, LRU_WIDTH), lambda bi, i: (bi, i, 0)),
        pl.BlockSpec((1, tm, 2 * D_MODEL), lambda bi, i: (bi, i, 0)),
    )
    consts = (p['wa'], p['wb'], p['gqa'], p['gkva'], p['wuq_t'], p['wukv_t'], p['gq'], p['gk'])
    in_specs = [
        pl.BlockSpec((1, tm, d), lambda bi, i: (bi, i, 0)),
        _mod_spec(mod_row), _mod_spec(mod_row),
        *[_const_spec(c.shape) for c in consts],
        pl.BlockSpec((2 * ROPE_PAIRS, tm), lambda bi, i: (0, i)),
        pl.BlockSpec((2 * ROPE_PAIRS, tm), lambda bi, i: (0, i)),
    ]
    args = [x, sh, sc, *consts, cos_t, sin_t]
    aliases = {}
    if kv_buffers is not None:
        aliases = {len(args): 1, len(args) + 1: 2}
        in_specs += [pl.BlockSpec(memory_space=pl.ANY), pl.BlockSpec(memory_space=pl.ANY)]
        args += list(kv_buffers)
    n_in = len(args)

    def body(*refs):
        _proj_kernel(*refs[:n_in - len(aliases)], *refs[n_in:])

    return pl.pallas_call(
        body,
        grid=(b, t // tm),
        in_specs=in_specs,
        out_specs=out_specs,
        out_shape=out_shape,
        input_output_aliases=aliases,
        compiler_params=_cparams(("arbitrary", "arbitrary")),
        name="proj",
    )(*args)


def _col_max(s):
    rows = s.shape[0]
    while rows > 4 * SUBLANES and rows % 2 == 0:
        rows //= 2
        s = jnp.maximum(s[:rows], s[rows:])
    return jnp.max(s, axis=0, keepdims=True)


def _attn_kernel(q_ref, k_ref, v_ref, o_ref, s_scr, mt_scr, *, n_steps, group, unroll):
    tq = q_ref.shape[2]
    n_blocks = tq // Q_BLOCK_LANES
    step_keys = group * KV_TILE
    q_blocks = [q_ref[0, :, Q_BLOCK_LANES * i:Q_BLOCK_LANES * (i + 1)] for i in range(n_blocks)]

    def k_step(j):
        off = j * step_keys if isinstance(j, int) else pl.multiple_of(j * step_keys, step_keys)
        return k_ref[0, 0, pl.ds(off, step_keys), :]

    def v_step(j):
        tiles = [v_ref[0, 0, group * j + t] for t in range(group)]
        return tiles[0] if group == 1 else jnp.concatenate(tiles, axis=1)

    def scores(j, slot):
        kt = k_step(j)
        for b, q_b in enumerate(q_blocks):
            s = _dot(kt, q_b)
            s_scr[slot, b] = s
            mt_scr[slot, b] = _col_max(s)

    def update(slot, j, state):
        vt = v_step(j)
        out = []
        for b, (m, acc) in enumerate(state):
            m_new = jnp.maximum(m, mt_scr[slot, b])
            p = jnp.exp2(s_scr[slot, b] - m_new).astype(BF16)
            alpha = jnp.exp2(m - m_new)
            out.append((m_new, alpha * acc + _dot(vt, p)))
        return tuple(out)

    n_pairs = (n_steps - 1) // 2
    state = tuple((jnp.full((1, Q_BLOCK_LANES), -jnp.inf, F32), jnp.zeros((V_AUG, Q_BLOCK_LANES), F32))
                  for _ in range(n_blocks))
    scores(0, 0)
    if n_pairs:
        def body(g, st):
            j = 2 * g
            scores(j + 1, 1)
            st = update(0, j, st)
            scores(j + 2, 0)
            return update(1, j + 1, st)
        state = lax.fori_loop(0, n_pairs, body, state, unroll=math.gcd(unroll, n_pairs))
    for j in range(2 * n_pairs, n_steps):
        slot = j % 2
        if j + 1 < n_steps:
            scores(j + 1, 1 - slot)
        state = update(slot, j, state)
    for i, (_, acc) in enumerate(state):
        o_ref[0, :, Q_BLOCK_LANES * i:Q_BLOCK_LANES * (i + 1)] = (
            acc[:V_HEAD] / acc[V_HEAD:V_HEAD + 1]).astype(BF16)


def _attn_call(q_t, k, v, first_tile, n_tiles, tq, group=1, unroll=1):
    b, hp, t = q_t.shape
    tk = n_tiles * KV_TILE
    assert first_tile % n_tiles == 0 and n_tiles % group == 0
    blk = first_tile // n_tiles
    n_blocks = tq // Q_BLOCK_LANES
    return pl.pallas_call(
        functools.partial(_attn_kernel, n_steps=n_tiles // group, group=group, unroll=unroll),
        grid=(b, N_HEADS, t // tq),
        in_specs=[
            pl.BlockSpec((1, HEAD_PAD, tq), lambda bi, h, i: (bi, h, i)),
            pl.BlockSpec((1, 1, tk, HEAD_PAD), lambda bi, h, i: (bi, h, blk, 0)),
            pl.BlockSpec((1, 1, n_tiles, V_AUG, KV_TILE), lambda bi, h, i: (bi, h, blk, 0, 0)),
        ],
        out_specs=pl.BlockSpec((1, V_HEAD, tq), lambda bi, h, i: (bi, h, i)),
        out_shape=jax.ShapeDtypeStruct((b, N_HEADS * V_HEAD, t), BF16),
        scratch_shapes=[
            pltpu.VMEM((2, n_blocks, group * KV_TILE, Q_BLOCK_LANES), F32),
            pltpu.VMEM((2, n_blocks, 1, Q_BLOCK_LANES), F32),
        ],
        compiler_params=_cparams(("arbitrary", "arbitrary", "arbitrary")),
        name="attn",
    )(q_t, k, v)


def _gelu_tanh(x):
    return 0.5 * x * (1.0 + jnp.tanh(math.sqrt(2.0 / math.pi) * (x + 0.044715 * (x * x * x))))


def _short_conv(cur, prev8, next8, xbuf, cw, cb):
    chunk = cur.shape[0]
    halo = SUBLANES
    xbuf[0:halo] = prev8
    xbuf[halo:halo + chunk] = cur
    xbuf[halo + chunk:2 * halo + chunk] = next8
    xc = cb
    for k in range(CONV_W):
        o = halo - 2 + k
        xc = xc + cw[k:k + 1] * xbuf[o:o + chunk]
    return xc


def _lru_kernel(*refs, reverse, n_chunks, chunk, combine):
    width = LRU_WIDTH
    s = pl.program_id(1)
    is_ctx = s == 0
    c = jnp.clip((n_chunks - s) if reverse else (s - 1), 0, n_chunks - 1)
    if combine:
        (xcc_ref, xcl_ref, wg_ref, bg_ref, lam_ref, hbc_ref, hbl_ref, uyc_ref, uy_ref,
         outc_ref, outl_ref, a_s, b_s, h_s, carry_s) = refs
        xc = jnp.where(is_ctx, xcc_ref[0], xcl_ref[0])
    else:
        (uxc_ref, ux_ref, uxp_ref, uxn_ref, cw_ref, cb_ref, wg_ref, bg_ref, lam_ref,
         outc_ref, outl_ref, xcc_ref, xcl_ref, a_s, b_s, h_s, carry_s, xbuf) = refs
        zero_h = jnp.zeros((SUBLANES, width), F32)
        prev8 = jnp.where(is_ctx | (c == 0), zero_h, uxp_ref[0])
        next8 = jnp.where(is_ctx | (c == n_chunks - 1), zero_h, uxn_ref[0])
        cur = jnp.where(is_ctx, uxc_ref[0], ux_ref[0])
        xc = _short_conv(cur, prev8, next8, xbuf, cw_ref[...], cb_ref[...])

    xcb = xc.astype(BF16)
    half = width // 2
    d0 = _dot(xcb[:, :half], wg_ref[0])
    d1 = _dot(xcb[:, half:], wg_ref[1])
    bg = bg_ref[...]
    r = jax.nn.sigmoid(jnp.concatenate([d0[:, :half], d1[:, :half]], axis=1) + bg[:, :width])
    i = jax.nn.sigmoid(jnp.concatenate([d0[:, half:], d1[:, half:]], axis=1) + bg[:, width:])
    nl = -lam_ref[...]
    softplus = jnp.maximum(nl, 0.0) + jnp.log(1.0 + jnp.exp(-jnp.abs(nl)))
    a = jnp.exp(-RG_C * r * softplus)
    a_s[...] = a
    v = 1.0 - a * a
    b_s[...] = jnp.where(v > 0.0, v * lax.rsqrt(v), 0.0) * (i * xc)

    row = lax.broadcasted_iota(jnp.int32, (SUBLANES, width), 0)
    n_groups = chunk // SUBLANES

    def body(g, carry):
        gg = (n_groups - 1 - g) if reverse else g
        off = pl.multiple_of(gg * SUBLANES, SUBLANES)
        av = a_s[pl.ds(off, SUBLANES), :]
        bv = b_s[pl.ds(off, SUBLANES), :]
        for k in (1, 2, 4):
            shift = (SUBLANES - k) if reverse else k
            valid = (row < SUBLANES - k) if reverse else (row >= k)
            a_sh = jnp.where(valid, pltpu.roll(av, shift, 0), 1.0)
            b_sh = jnp.where(valid, pltpu.roll(bv, shift, 0), 0.0)
            bv = av * b_sh + bv
            av = av * a_sh
        hblk = av * carry + bv
        h_s[pl.ds(off, SUBLANES), :] = hblk
        return hblk[0:1] if reverse else hblk[SUBLANES - 1:SUBLANES]

    carry0 = jnp.where(is_ctx, jnp.zeros((1, width), F32), carry_s[...])
    carry_s[...] = lax.fori_loop(0, n_groups, body, carry0)
    h = h_s[...]

    if combine:
        hb = jnp.where(is_ctx, hbc_ref[0], hbl_ref[0])
        uy = jnp.where(is_ctx, uyc_ref[0], uy_ref[0])
        val = ((h + hb) * _gelu_tanh(uy)).astype(BF16)
    else:
        val = h

    @pl.when(is_ctx)
    def _():
        outc_ref[0] = val
        if not combine:
            xcc_ref[0] = xc

    @pl.when(jnp.logical_not(is_ctx))
    def _():
        outl_ref[0] = val
        if not combine:
            xcl_ref[0] = xc


def _lru_call(ux_c, ux_l, p, direction, reverse, combine_with=None):
    b, s_len, w = ux_l.shape
    c_len = ux_c.shape[1]
    chunk = c_len
    assert s_len % chunk == 0 and chunk % SUBLANES == 0
    n = s_len // chunk
    hpc = chunk // SUBLANES
    n_hblk = s_len // SUBLANES

    def cidx(s):
        return jnp.clip((n - s) if reverse else (s - 1), 0, n - 1)

    ctx_spec = pl.BlockSpec((1, c_len, w), lambda bi, s: (bi, 0, 0))
    lat_spec = pl.BlockSpec((1, chunk, w), lambda bi, s: (bi, cidx(s), 0))
    gate_consts = (p['wg'][direction], p['bg'][direction], p['lam'][direction])
    combine = combine_with is not None
    if combine:
        consts = gate_consts
        in_specs = [ctx_spec, lat_spec, *[_const_spec(cst.shape) for cst in consts],
                    ctx_spec, lat_spec, ctx_spec, lat_spec]
        args = [ux_c, ux_l, *consts, *combine_with]
        out_specs = (ctx_spec, lat_spec)
        out_shape = (jax.ShapeDtypeStruct(ux_c.shape, BF16), jax.ShapeDtypeStruct(ux_l.shape, BF16))
    else:
        consts = (p['conv_w'], p['conv_b'], *gate_consts)
        in_specs = [
            ctx_spec, lat_spec,
            pl.BlockSpec((1, SUBLANES, w), lambda bi, s: (bi, jnp.maximum(cidx(s) * hpc - 1, 0), 0)),
            pl.BlockSpec((1, SUBLANES, w), lambda bi, s: (bi, jnp.minimum((cidx(s) + 1) * hpc, n_hblk - 1), 0)),
            *[_const_spec(cst.shape) for cst in consts],
        ]
        args = [ux_c, ux_l, ux_l, ux_l, *consts]
        out_specs = (ctx_spec, lat_spec, ctx_spec, lat_spec)
        out_shape = tuple(jax.ShapeDtypeStruct(a.shape, F32) for a in (ux_c, ux_l, ux_c, ux_l))
    scratch = [
        pltpu.VMEM((chunk, w), F32), pltpu.VMEM((chunk, w), F32), pltpu.VMEM((chunk, w), F32),
        pltpu.VMEM((1, w), F32),
    ]
    if not combine:
        scratch.append(pltpu.VMEM((chunk + 2 * SUBLANES, w), F32))
    return pl.pallas_call(
        functools.partial(_lru_kernel, reverse=reverse, n_chunks=n, chunk=chunk, combine=combine),
        grid=(b, n + 1),
        in_specs=in_specs,
        out_specs=out_specs,
        out_shape=out_shape,
        scratch_shapes=scratch,
        compiler_params=_cparams(("arbitrary", "arbitrary")),
        name="lru_fwd" if combine else "lru_bwd",
    )(*args)


def _merge_kernel(x_ref, att_ref, lru_ref, gs_ref, g_ref, woa_ref, wol_ref, wout_ref, o_ref):
    att = lax.dot_general(att_ref[0], woa_ref[...], TN_DIMS, preferred_element_type=F32)
    lru = _dot(lru_ref[0], wol_ref[...])
    gs = gs_ref[0].astype(F32)
    merged = (gs[:, :D_MODEL] * att + gs[:, D_MODEL:] * lru).astype(BF16)
    o_ref[0] = x_ref[0] + g_ref[0] * _dot(merged, wout_ref[...])


def _merge_call(x, att_t, lru, gs, g, p, mod_row, tm):
    b, t, d = x.shape
    consts = (p['w_o_attn'], p['w_o_lru'], p['w_out'])
    return pl.pallas_call(
        _merge_kernel,
        grid=(b, t // tm),
        in_specs=[
            pl.BlockSpec((1, tm, d), lambda bi, i: (bi, i, 0)),
            pl.BlockSpec((1, N_HEADS * V_HEAD, tm), lambda bi, i: (bi, 0, i)),
            pl.BlockSpec((1, tm, LRU_WIDTH), lambda bi, i: (bi, i, 0)),
            pl.BlockSpec((1, tm, 2 * D_MODEL), lambda bi, i: (bi, i, 0)),
            _mod_spec(mod_row),
            *[_const_spec(c.shape) for c in consts],
        ],
        out_specs=pl.BlockSpec((1, tm, d), lambda bi, i: (bi, i, 0)),
        out_shape=jax.ShapeDtypeStruct(x.shape, F32),
        compiler_params=_cparams(("arbitrary", "arbitrary")),
        name="merge",
    )(x, att_t, lru, gs, g, *consts)


def _block_diag_gates(w_r, w_i):
    def bd(w4):
        z = jnp.zeros((4, LRU_BLOCK, 4, LRU_BLOCK), w4.dtype)
        idx = jnp.arange(4)
        z = z.at[idx, :, idx, :].set(w4)
        return z.reshape(4 * LRU_BLOCK, 4 * LRU_BLOCK)
    groups = []
    for j in range(2):
        groups.append(jnp.concatenate([bd(w_r[4 * j:4 * j + 4]), bd(w_i[4 * j:4 * j + 4])], axis=1))
    return jnp.stack(groups).astype(BF16)


def _layer_params(l, w):
    w_in = w['w_in'][l]
    split = Q_LORA + KV_LORA + QK_ROPE
    wa = jnp.pad(w_in[:, :split], ((0, 0), (0, PROJ_A - split))).astype(BF16)
    wb = w_in[:, split:].astype(BF16)
    w_ukv = w['w_ukv'][l].reshape(KV_LORA, N_HEADS, QK_NOPE + V_HEAD)
    wukv_t = jnp.concatenate([
        w_ukv[:, :, :QK_NOPE].reshape(KV_LORA, N_HEADS * QK_NOPE),
        w_ukv[:, :, QK_NOPE:].reshape(KV_LORA, N_HEADS * V_HEAD)], axis=1).T.astype(BF16)
    return {
        'ff1_w_in': w['ff1_w_in'][l].astype(BF16), 'ff1_w_out': w['ff1_w_out'][l].astype(BF16),
        'ff2_w_in': w['ff2_w_in'][l].astype(BF16), 'ff2_w_out': w['ff2_w_out'][l].astype(BF16),
        'wa': wa, 'wb': wb,
        'gqa': w['q_a_norm'][l].reshape(1, Q_LORA), 'gkva': w['kv_a_norm'][l].reshape(1, KV_LORA),
        'wuq_t': w['w_uq'][l].T.astype(BF16), 'wukv_t': wukv_t,
        'gq': (w['q_norm'][l] * (ATTN_SCALE * math.log2(math.e))).reshape(QK_HEAD, 1),
        'gk': w['k_norm'][l].reshape(QK_HEAD, 1),
        'conv_w': w['conv_w'][l], 'conv_b': w['conv_b'][l].reshape(1, LRU_WIDTH),
        'wg': [_block_diag_gates(w['w_rgate'][l, d], w['w_igate'][l, d]) for d in range(2)],
        'bg': [jnp.concatenate([w['b_rgate'][l, d], w['b_igate'][l, d]]).reshape(1, 2 * LRU_WIDTH)
               for d in range(2)],
        'lam': [w['lru_lambda'][l, d].reshape(1, LRU_WIDTH) for d in range(2)],
        'w_o_attn': w['w_o_attn'][l].astype(BF16), 'w_o_lru': w['w_o_lru'][l].astype(BF16),
        'w_out': w['w_out'][l].astype(BF16),
    }


def _rope_tables_t(n_tokens):
    rows = n_tokens // GRID_W
    row = jnp.repeat(jnp.arange(rows, dtype=F32), GRID_W)
    col = jnp.tile(jnp.arange(GRID_W, dtype=F32), rows)
    inv_freq = ROPE_BASE ** (-jnp.arange(ROPE_PAIRS, dtype=F32) / ROPE_PAIRS)
    ang = jnp.concatenate([inv_freq[:, None] * row[None, :], inv_freq[:, None] * col[None, :]], axis=0)
    return jnp.cos(ang), jnp.sin(ang)


def kernel(x, c, ctx, c_ctx, w_ada, b_ada, ff1_w_in, ff1_w_out, ff2_w_in, ff2_w_out, w_in, q_a_norm, w_uq,
           kv_a_norm, w_ukv, q_norm, k_norm, conv_w, conv_b, lru_lambda, w_rgate, b_rgate, w_igate, b_igate,
           w_o_attn, w_o_lru, w_out):
    weights = dict(ff1_w_in=ff1_w_in, ff1_w_out=ff1_w_out, ff2_w_in=ff2_w_in, ff2_w_out=ff2_w_out, w_in=w_in,
                   q_a_norm=q_a_norm, w_uq=w_uq, kv_a_norm=kv_a_norm, w_ukv=w_ukv, q_norm=q_norm, k_norm=k_norm,
                   conv_w=conv_w, conv_b=conv_b, lru_lambda=lru_lambda, w_rgate=w_rgate, b_rgate=b_rgate,
                   w_igate=w_igate, b_igate=b_igate, w_o_attn=w_o_attn, w_o_lru=w_o_lru, w_out=w_out)
    batch, seq, d = x.shape
    c_len = ctx.shape[1]
    depth = w_ada.shape[0]
    assert d == D_MODEL and c_len == KV_TILE and seq % KV_TILE == 0 and batch < SUBLANES
    ctx_row = batch
    tm = 256
    tm_ffn = min(512, seq)
    tq = min(1024, seq)

    cond_raw = jnp.zeros((SUBLANES, d), F32).at[:batch].set(c).at[ctx_row].set(c_ctx)
    mods = _ada_call(cond_raw, w_ada, b_ada)
    mods = mods.reshape(depth, SUBLANES, N_MOD, 1, d).transpose(0, 2, 1, 3, 4)

    cos_l, sin_l = _rope_tables_t(seq)
    cos_c = jnp.ones((2 * ROPE_PAIRS, c_len), F32)
    sin_c = jnp.zeros((2 * ROPE_PAIRS, c_len), F32)

    xl, xc = x, ctx
    for l in range(depth):
        p = _layer_params(l, weights)
        m = mods[l]
        last = l == depth - 1
        xl = _ffn_call(xl, m[0], m[1], m[2], p['ff1_w_in'], p['ff1_w_out'], None, tm_ffn)
        xc = _ffn_call(xc, m[0], m[1], m[2], p['ff1_w_in'], p['ff1_w_out'], ctx_row, c_len)
        kv_tokens = seq + c_len
        qt_l, k_all, v_all, ux_l, uy_l, gs_l = _proj_call(xl, m[3], m[4], p, cos_l, sin_l, None, tm, kv_tokens, 0)
        qt_c, k_all, v_all, ux_c, uy_c, gs_c = _proj_call(xc, m[3], m[4], p, cos_c, sin_c, ctx_row, c_len,
                                                          kv_tokens, seq, kv_buffers=(k_all, v_all))
        n_kv_tiles = kv_tokens // KV_TILE
        group = ATTN_GROUP if n_kv_tiles % ATTN_GROUP == 0 else 1
        att_l = _attn_call(qt_l, k_all, v_all, 0, n_kv_tiles, tq, group=group, unroll=ATTN_PAIR_UNROLL)
        hb_c, hb_l, xv_c, xv_l = _lru_call(ux_c, ux_l, p, 1, True)
        lru_c, lru_l = _lru_call(xv_c, xv_l, p, 0, False, combine_with=(hb_c, hb_l, uy_c, uy_l))
        xl = _merge_call(xl, att_l, lru_l, gs_l, m[5], p, None, tm_ffn)
        xl = _ffn_call(xl, m[6], m[7], m[8], p['ff2_w_in'], p['ff2_w_out'], None, tm_ffn)
        if not last:
            att_c = _attn_call(qt_c, k_all, v_all, seq // KV_TILE, c_len // KV_TILE, c_len)
            xc = _merge_call(xc, att_c, lru_c, gs_c, m[5], p, ctx_row, c_len)
            xc = _ffn_call(xc, m[6], m[7], m[8], p['ff2_w_in'], p['ff2_w_out'], ctx_row, c_len)
    return xl
```

```python
import functools
import math

import jax
import jax.numpy as jnp
from jax import lax
from jax.experimental import pallas as pl
from jax.experimental.pallas import tpu as pltpu

F32 = jnp.float32
BF16 = jnp.bfloat16

D_MODEL = 1024
N_HEADS = 8
QK_NOPE = 64
QK_ROPE = 32
QK_HEAD = QK_NOPE + QK_ROPE
V_HEAD = 64
Q_LORA = 384
KV_LORA = 256
LRU_WIDTH = 512
LRU_BLOCKS = 8
LRU_BLOCK = LRU_WIDTH // LRU_BLOCKS
CONV_W = 4
RG_C = 8.0
D_FF = 2816
N_MOD = 9
EPS = 1e-6
GRID_W = 64
ROPE_PAIRS = QK_ROPE // 4
ROPE_BASE = 10000.0
ATTN_SCALE = QK_HEAD ** -0.5

LANES = 128
SUBLANES = 8
MXU_DIM = 256
VMEM_LIMIT_BYTES = 60 * 1024 * 1024

HEAD_PAD = LANES
V_AUG = V_HEAD + 16
Q_BLOCK_LANES = MXU_DIM
KV_TILE = MXU_DIM
PROJ_A = 768
PROJ_B = 2 * LRU_WIDTH + 2 * D_MODEL
ADA_TN = 1152
ATTN_TILES_PER_BODY = 16
SCORE_BOUND_LOG2 = 60.0
ATTN_GROUP = 1
ATTN_PAIR_UNROLL = 4

NT_DIMS = (((1,), (1,)), ((), ()))
TN_DIMS = (((0,), (0,)), ((), ()))


def _dot(a, b):
    return jnp.dot(a, b, preferred_element_type=F32)


def _rms_rows(x):
    return x * lax.rsqrt(jnp.mean(x * x, axis=-1, keepdims=True) + EPS)


def _cparams(sem, flags=None):
    return pltpu.CompilerParams(dimension_semantics=sem, vmem_limit_bytes=VMEM_LIMIT_BYTES, flags=flags)


def _const_spec(shape):
    nd = len(shape)
    return pl.BlockSpec(shape, lambda *_: (0,) * nd, pipeline_mode=pl.Buffered(1))


def _ada_kernel(c_ref, w_ref, b_ref, o_ref):
    c = c_ref[...]
    cond = c * jax.nn.sigmoid(c)
    o_ref[0] = jnp.dot(cond, w_ref[0], preferred_element_type=F32,
                       precision=lax.Precision.HIGHEST) + b_ref[0]


def _ada_call(cond_raw, w_ada, b_ada):
    depth, d, n = w_ada.shape
    rows = cond_raw.shape[0]
    return pl.pallas_call(
        _ada_kernel,
        grid=(depth, n // ADA_TN),
        in_specs=[
            pl.BlockSpec((rows, d), lambda l, j: (0, 0)),
            pl.BlockSpec((1, d, ADA_TN), lambda l, j: (l, 0, j)),
            pl.BlockSpec((1, 1, ADA_TN), lambda l, j: (l, 0, j)),
        ],
        out_specs=pl.BlockSpec((1, rows, ADA_TN), lambda l, j: (l, 0, j)),
        out_shape=jax.ShapeDtypeStruct((depth, rows, n), F32),
        compiler_params=_cparams(("arbitrary", "arbitrary")),
        name="ada",
    )(cond_raw, w_ada, b_ada.reshape(depth, 1, n))


def _mod_spec(mod_row):
    if mod_row is None:
        return pl.BlockSpec((1, 1, D_MODEL), lambda b, i: (b, 0, 0))
    return pl.BlockSpec((1, 1, D_MODEL), lambda b, i: (mod_row, 0, 0))


def _ffn_kernel(x_ref, sh_ref, sc_ref, g_ref, win_ref, wout_ref, o_ref):
    x = x_ref[0]
    xb = (_rms_rows(x) * (1.0 + sc_ref[0]) + sh_ref[0]).astype(BF16)
    g = _dot(xb, win_ref[:, :D_FF])
    u = _dot(xb, win_ref[:, D_FF:])
    a = (g * jax.nn.sigmoid(g) * u).astype(BF16)
    y = _dot(a, wout_ref[...])
    o_ref[0] = x + (0.5 * g_ref[0]) * y


def _ffn_call(x, sh, sc, g, w_in, w_out, mod_row, tm):
    b, t, d = x.shape
    return pl.pallas_call(
        _ffn_kernel,
        grid=(b, t // tm),
        in_specs=[
            pl.BlockSpec((1, tm, d), lambda bi, i: (bi, i, 0)),
            _mod_spec(mod_row), _mod_spec(mod_row), _mod_spec(mod_row),
            _const_spec(w_in.shape), _const_spec(w_out.shape),
        ],
        out_specs=pl.BlockSpec((1, tm, d), lambda bi, i: (bi, i, 0)),
        out_shape=jax.ShapeDtypeStruct(x.shape, F32),
        compiler_params=_cparams(("arbitrary", "arbitrary")),
        name="ffn",
    )(x, sh, sc, g, w_in, w_out)


def _rope_t(t, cos, sin):
    p = ROPE_PAIRS
    x1r, x2r, x1c, x2c = t[0:p], t[p:2 * p], t[2 * p:3 * p], t[3 * p:4 * p]
    cr, cc = cos[0:p], cos[p:2 * p]
    sr, sc = sin[0:p], sin[p:2 * p]
    return jnp.concatenate(
        [x1r * cr - x2r * sr, x1r * sr + x2r * cr, x1c * cc - x2c * sc, x1c * sc + x2c * cc], axis=0)


def _proj_kernel(x_ref, sh_ref, sc_ref, wa_ref, wb_ref, gqa_ref, gkva_ref, wuq_ref, wukv_ref,
                 gq_ref, gk_ref, cos_ref, sin_ref,
                 qt_ref, k_ref, v_ref, ux_ref, uy_ref, gs_ref):
    tm = x_ref.shape[1]
    x = x_ref[0]
    xb = (_rms_rows(x) * (1.0 + sc_ref[0]) + sh_ref[0]).astype(BF16)
    ha = _dot(xb, wa_ref[...])
    hb = _dot(xb, wb_ref[...])
    ux_ref[0] = hb[:, :LRU_WIDTH]
    uy_ref[0] = hb[:, LRU_WIDTH:2 * LRU_WIDTH]
    gs_ref[0] = jax.nn.sigmoid(hb[:, 2 * LRU_WIDTH:]).astype(BF16)

    qn = (_rms_rows(ha[:, :Q_LORA]) * gqa_ref[...]).astype(BF16)
    kvn = (_rms_rows(ha[:, Q_LORA:Q_LORA + KV_LORA]) * gkva_ref[...]).astype(BF16)
    q_t = lax.dot_general(wuq_ref[...], qn, NT_DIMS, preferred_element_type=F32)
    kv_t = lax.dot_general(wukv_ref[...], kvn, NT_DIMS, preferred_element_type=F32)
    kr_t = ha[:, Q_LORA + KV_LORA:].T[:QK_ROPE]

    cos = cos_ref[...]
    sin = sin_ref[...]
    gq = gq_ref[...]
    gk = gk_ref[...]
    kr_rot = _rope_t(kr_t * gk[QK_NOPE:], cos, sin)
    kr_ss = jnp.sum(kr_t * kr_t, axis=0, keepdims=True)
    zpad = jnp.zeros((HEAD_PAD - QK_HEAD, tm), F32)
    ones_rows = jnp.where(lax.broadcasted_iota(jnp.int32, (V_AUG - V_HEAD, tm), 0) == 0, 1.0, 0.0)
    inv_d = 1.0 / QK_HEAD
    for h in range(N_HEADS):
        q = q_t[QK_HEAD * h:QK_HEAD * (h + 1)]
        rq = lax.rsqrt(jnp.sum(q * q, axis=0, keepdims=True) * inv_d + EPS)
        qs = q * rq * gq
        qh = jnp.concatenate([qs[:QK_NOPE], _rope_t(qs[QK_NOPE:], cos, sin), zpad], axis=0)
        qt_ref[0, HEAD_PAD * h:HEAD_PAD * (h + 1), :] = qh.astype(BF16)

        kn = kv_t[QK_NOPE * h:QK_NOPE * (h + 1)]
        rk = lax.rsqrt((jnp.sum(kn * kn, axis=0, keepdims=True) + kr_ss) * inv_d + EPS)
        kh = jnp.concatenate([kn * gk[:QK_NOPE] * rk, kr_rot * rk, zpad], axis=0)
        k_ref[0, h] = kh.T.astype(BF16)

        v0 = N_HEADS * QK_NOPE + V_HEAD * h
        vt = jnp.concatenate([kv_t[v0:v0 + V_HEAD], ones_rows], axis=0).astype(BF16)
        for c in range(tm // KV_TILE):
            v_ref[0, h, c] = vt[:, KV_TILE * c:KV_TILE * (c + 1)]


def _proj_call(x, sh, sc, p, cos_t, sin_t, mod_row, tm, kv_tokens, kv_offset, kv_buffers=None):
    b, t, d = x.shape
    off_blk = kv_offset // tm
    out_shape = (
        jax.ShapeDtypeStruct((b, N_HEADS * HEAD_PAD, t), BF16),
        jax.ShapeDtypeStruct((b, N_HEADS, kv_tokens, HEAD_PAD), BF16),
        jax.ShapeDtypeStruct((b, N_HEADS, kv_tokens // KV_TILE, V_AUG, KV_TILE), BF16),
        jax.ShapeDtypeStruct((b, t, LRU_WIDTH), F32),
        jax.ShapeDtypeStruct((b, t, LRU_WIDTH), F32),
        jax.ShapeDtypeStruct((b, t, 2 * D_MODEL), BF16),
    )
    out_specs = (
        pl.BlockSpec((1, N_HEADS * HEAD_PAD, tm), lambda bi, i: (bi, 0, i)),
        pl.BlockSpec((1, N_HEADS, tm, HEAD_PAD), lambda bi, i: (bi, 0, off_blk + i, 0)),
        pl.BlockSpec((1, N_HEADS, tm // KV_TILE, V_AUG, KV_TILE), lambda bi, i: (bi, 0, off_blk + i, 0, 0)),
        pl.BlockSpec((1, tm, LRU_WIDTH), lambda bi, i: (bi, i, 0)),
        pl.BlockSpec((1, tm, LRU_WIDTH), lambda bi, i: (bi, i, 0)),
        pl.BlockSpec((1, tm, 2 * D_MODEL), lambda bi, i: (bi, i, 0)),
    )
    consts = (p['wa'], p['wb'], p['gqa'], p['gkva'], p['wuq_t'], p['wukv_t'], p['gq'], p['gk'])
    in_specs = [
        pl.BlockSpec((1, tm, d), lambda bi, i: (bi, i, 0)),
        _mod_spec(mod_row), _mod_spec(mod_row),
        *[_const_spec(c.shape) for c in consts],
        pl.BlockSpec((2 * ROPE_PAIRS, tm), lambda bi, i: (0, i)),
        pl.BlockSpec((2 * ROPE_PAIRS, tm), lambda bi, i: (0, i)),
    ]
    args = [x, sh, sc, *consts, cos_t, sin_t]
    aliases = {}
    if kv_buffers is not None:
        aliases = {len(args): 1, len(args) + 1: 2}
        in_specs += [pl.BlockSpec(memory_space=pl.ANY), pl.BlockSpec(memory_space=pl.ANY)]
        args += list(kv_buffers)
    n_in = len(args)

    def body(*refs):
        _proj_kernel(*refs[:n_in - len(aliases)], *refs[n_in:])

    return pl.pallas_call(
        body,
        grid=(b, t // tm),
        in_specs=in_specs,
        out_specs=out_specs,
        out_shape=out_shape,
        input_output_aliases=aliases,
        compiler_params=_cparams(("arbitrary", "arbitrary")),
        name="proj",
    )(*args)


def _col_max(s):
    rows = s.shape[0]
    while rows > 4 * SUBLANES and rows % 2 == 0:
        rows //= 2
        s = jnp.maximum(s[:rows], s[rows:])
    return jnp.max(s, axis=0, keepdims=True)


def _attn_kernel(q_ref, k_ref, v_ref, o_ref, s_scr, mt_scr, *, n_steps, group, unroll):
    tq = q_ref.shape[2]
    n_blocks = tq // Q_BLOCK_LANES
    step_keys = group * KV_TILE
    q_blocks = [q_ref[0, :, Q_BLOCK_LANES * i:Q_BLOCK_LANES * (i + 1)] for i in range(n_blocks)]

    def k_step(j):
        off = j * step_keys if isinstance(j, int) else pl.multiple_of(j * step_keys, step_keys)
        return k_ref[0, 0, pl.ds(off, step_keys), :]

    def v_step(j):
        tiles = [v_ref[0, 0, group * j + t] for t in range(group)]
        return tiles[0] if group == 1 else jnp.concatenate(tiles, axis=1)

    def scores(j, slot):
        kt = k_step(j)
        for b, q_b in enumerate(q_blocks):
            s = _dot(kt, q_b)
            s_scr[slot, b] = s
            mt_scr[slot, b] = _col_max(s)

    def update(slot, j, state):
        vt = v_step(j)
        out = []
        for b, (m, acc) in enumerate(state):
            m_new = jnp.maximum(m, mt_scr[slot, b])
            p = jnp.exp2(s_scr[slot, b] - m_new).astype(BF16)
            alpha = jnp.exp2(m - m_new)
            out.append((m_new, alpha * acc + _dot(vt, p)))
        return tuple(out)

    n_pairs = (n_steps - 1) // 2
    state = tuple((jnp.full((1, Q_BLOCK_LANES), -jnp.inf, F32), jnp.zeros((V_AUG, Q_BLOCK_LANES), F32))
                  for _ in range(n_blocks))
    scores(0, 0)
    if n_pairs:
        def body(g, st):
            j = 2 * g
            scores(j + 1, 1)
            st = update(0, j, st)
            scores(j + 2, 0)
            return update(1, j + 1, st)
        state = lax.fori_loop(0, n_pairs, body, state, unroll=math.gcd(unroll, n_pairs))
    for j in range(2 * n_pairs, n_steps):
        slot = j % 2
        if j + 1 < n_steps:
            scores(j + 1, 1 - slot)
        state = update(slot, j, state)
    for i, (_, acc) in enumerate(state):
        o_ref[0, :, Q_BLOCK_LANES * i:Q_BLOCK_LANES * (i + 1)] = (
            acc[:V_HEAD] / acc[V_HEAD:V_HEAD + 1]).astype(BF16)


def _attn_call(q_t, k, v, first_tile, n_tiles, tq, group=1, unroll=1):
    b, hp, t = q_t.shape
    tk = n_tiles * KV_TILE
    assert first_tile % n_tiles == 0 and n_tiles % group == 0
    blk = first_tile // n_tiles
    n_blocks = tq // Q_BLOCK_LANES
    return pl.pallas_call(
        functools.partial(_attn_kernel, n_steps=n_tiles // group, group=group, unroll=unroll),
        grid=(b, N_HEADS, t // tq),
        in_specs=[
            pl.BlockSpec((1, HEAD_PAD, tq), lambda bi, h, i: (bi, h, i)),
            pl.BlockSpec((1, 1, tk, HEAD_PAD), lambda bi, h, i: (bi, h, blk, 0)),
            pl.BlockSpec((1, 1, n_tiles, V_AUG, KV_TILE), lambda bi, h, i: (bi, h, blk, 0, 0)),
        ],
        out_specs=pl.BlockSpec((1, V_HEAD, tq), lambda bi, h, i: (bi, h, i)),
        out_shape=jax.ShapeDtypeStruct((b, N_HEADS * V_HEAD, t), BF16),
        scratch_shapes=[
            pltpu.VMEM((2, n_blocks, group * KV_TILE, Q_BLOCK_LANES), F32),
            pltpu.VMEM((2, n_blocks, 1, Q_BLOCK_LANES), F32),
        ],
        compiler_params=_cparams(("arbitrary", "arbitrary", "arbitrary")),
        name="attn",
    )(q_t, k, v)


def _row_group_sum(p):
    rows = p.shape[0]
    while rows > SUBLANES and rows % 2 == 0:
        rows //= 2
        p = p[:rows] + p[rows:]
    return p


def _attn_bounded_kernel(q_ref, k_ref, v_ref, o_ref, s_scr, *, n_steps, tiles_per_body):
    tq = q_ref.shape[2]
    n_blocks = tq // Q_BLOCK_LANES
    q_blocks = [q_ref[0, :, Q_BLOCK_LANES * i:Q_BLOCK_LANES * (i + 1)] for i in range(n_blocks)]

    def k_step(j):
        off = j * KV_TILE if isinstance(j, int) else pl.multiple_of(j * KV_TILE, KV_TILE)
        return k_ref[0, 0, pl.ds(off, KV_TILE), :]

    def scores(j):
        kt = k_step(j)
        return [_dot(kt, q_b) for q_b in q_blocks]

    def accumulate(s_blocks, j, accs):
        vt = v_ref[0, 0, j][:V_HEAD]
        out = []
        for s, (num, den) in zip(s_blocks, accs):
            p = jnp.exp2(s)
            out.append((num + _dot(vt, p.astype(BF16)), den + _row_group_sum(p)))
        return tuple(out)

    accs = tuple((jnp.zeros((V_HEAD, Q_BLOCK_LANES), F32), jnp.zeros((SUBLANES, Q_BLOCK_LANES), F32))
                 for _ in range(n_blocks))
    n_bodies = (n_steps - 1) // tiles_per_body
    for b, s in enumerate(scores(0)):
        s_scr[b] = s
    if n_bodies:
        def body(g, accs):
            j0 = g * tiles_per_body
            s_cur = [s_scr[b] for b in range(n_blocks)]
            for t in range(tiles_per_body):
                s_next = scores(j0 + t + 1)
                accs = accumulate(s_cur, j0 + t, accs)
                s_cur = s_next
            for b, s in enumerate(s_cur):
                s_scr[b] = s
            return accs
        accs = lax.fori_loop(0, n_bodies, body, accs)
    s_cur = [s_scr[b] for b in range(n_blocks)]
    for j in range(n_bodies * tiles_per_body, n_steps):
        s_next = scores(j + 1) if j + 1 < n_steps else None
        accs = accumulate(s_cur, j, accs)
        s_cur = s_next
    for i, (num, den) in enumerate(accs):
        o_ref[0, :, Q_BLOCK_LANES * i:Q_BLOCK_LANES * (i + 1)] = (
            num / jnp.sum(den, axis=0, keepdims=True)).astype(BF16)


def _attn_bounded_call(q_t, k, v, n_tiles, tq, tiles_per_body):
    b, hp, t = q_t.shape
    n_blocks = tq // Q_BLOCK_LANES
    return pl.pallas_call(
        functools.partial(_attn_bounded_kernel, n_steps=n_tiles, tiles_per_body=tiles_per_body),
        grid=(b, N_HEADS, t // tq),
        in_specs=[
            pl.BlockSpec((1, HEAD_PAD, tq), lambda bi, h, i: (bi, h, i)),
            pl.BlockSpec((1, 1, n_tiles * KV_TILE, HEAD_PAD), lambda bi, h, i: (bi, h, 0, 0)),
            pl.BlockSpec((1, 1, n_tiles, V_AUG, KV_TILE), lambda bi, h, i: (bi, h, 0, 0, 0)),
        ],
        out_specs=pl.BlockSpec((1, V_HEAD, tq), lambda bi, h, i: (bi, h, i)),
        out_shape=jax.ShapeDtypeStruct((b, N_HEADS * V_HEAD, t), BF16),
        scratch_shapes=[pltpu.VMEM((n_blocks, KV_TILE, Q_BLOCK_LANES), F32)],
        compiler_params=_cparams(("arbitrary", "arbitrary", "arbitrary")),
        name="attn_bounded",
    )(q_t, k, v)


def _gelu_tanh(x):
    return 0.5 * x * (1.0 + jnp.tanh(math.sqrt(2.0 / math.pi) * (x + 0.044715 * (x * x * x))))


def _short_conv(cur, prev8, next8, xbuf, cw, cb):
    chunk = cur.shape[0]
    halo = SUBLANES
    xbuf[0:halo] = prev8
    xbuf[halo:halo + chunk] = cur
    xbuf[halo + chunk:2 * halo + chunk] = next8
    xc = cb
    for k in range(CONV_W):
        o = halo - 2 + k
        xc = xc + cw[k:k + 1] * xbuf[o:o + chunk]
    return xc


def _lru_kernel(*refs, reverse, n_chunks, chunk, combine):
    width = LRU_WIDTH
    s = pl.program_id(1)
    is_ctx = s == 0
    c = jnp.clip((n_chunks - s) if reverse else (s - 1), 0, n_chunks - 1)
    if combine:
        (xcc_ref, xcl_ref, wg_ref, bg_ref, lam_ref, hbc_ref, hbl_ref, uyc_ref, uy_ref,
         outc_ref, outl_ref, a_s, b_s, h_s, carry_s) = refs
        xc = jnp.where(is_ctx, xcc_ref[0], xcl_ref[0])
    else:
        (uxc_ref, ux_ref, uxp_ref, uxn_ref, cw_ref, cb_ref, wg_ref, bg_ref, lam_ref,
         outc_ref, outl_ref, xcc_ref, xcl_ref, a_s, b_s, h_s, carry_s, xbuf) = refs
        zero_h = jnp.zeros((SUBLANES, width), F32)
        prev8 = jnp.where(is_ctx | (c == 0), zero_h, uxp_ref[0])
        next8 = jnp.where(is_ctx | (c == n_chunks - 1), zero_h, uxn_ref[0])
        cur = jnp.where(is_ctx, uxc_ref[0], ux_ref[0])
        xc = _short_conv(cur, prev8, next8, xbuf, cw_ref[...], cb_ref[...])

    xcb = xc.astype(BF16)
    half = width // 2
    d0 = _dot(xcb[:, :half], wg_ref[0])
    d1 = _dot(xcb[:, half:], wg_ref[1])
    bg = bg_ref[...]
    r = jax.nn.sigmoid(jnp.concatenate([d0[:, :half], d1[:, :half]], axis=1) + bg[:, :width])
    i = jax.nn.sigmoid(jnp.concatenate([d0[:, half:], d1[:, half:]], axis=1) + bg[:, width:])
    nl = -lam_ref[...]
    softplus = jnp.maximum(nl, 0.0) + jnp.log(1.0 + jnp.exp(-jnp.abs(nl)))
    a = jnp.exp(-RG_C * r * softplus)
    a_s[...] = a
    v = 1.0 - a * a
    b_s[...] = jnp.where(v > 0.0, v * lax.rsqrt(v), 0.0) * (i * xc)

    row = lax.broadcasted_iota(jnp.int32, (SUBLANES, width), 0)
    n_groups = chunk // SUBLANES

    def body(g, carry):
        gg = (n_groups - 1 - g) if reverse else g
        off = pl.multiple_of(gg * SUBLANES, SUBLANES)
        av = a_s[pl.ds(off, SUBLANES), :]
        bv = b_s[pl.ds(off, SUBLANES), :]
        for k in (1, 2, 4):
            shift = (SUBLANES - k) if reverse else k
            valid = (row < SUBLANES - k) if reverse else (row >= k)
            a_sh = jnp.where(valid, pltpu.roll(av, shift, 0), 1.0)
            b_sh = jnp.where(valid, pltpu.roll(bv, shift, 0), 0.0)
            bv = av * b_sh + bv
            av = av * a_sh
        hblk = av * carry + bv
        h_s[pl.ds(off, SUBLANES), :] = hblk
        return hblk[0:1] if reverse else hblk[SUBLANES - 1:SUBLANES]

    carry0 = jnp.where(is_ctx, jnp.zeros((1, width), F32), carry_s[...])
    carry_s[...] = lax.fori_loop(0, n_groups, body, carry0)
    h = h_s[...]

    if combine:
        hb = jnp.where(is_ctx, hbc_ref[0], hbl_ref[0])
        uy = jnp.where(is_ctx, uyc_ref[0], uy_ref[0])
        val = ((h + hb) * _gelu_tanh(uy)).astype(BF16)
    else:
        val = h

    @pl.when(is_ctx)
    def _():
        outc_ref[0] = val
        if not combine:
            xcc_ref[0] = xc

    @pl.when(jnp.logical_not(is_ctx))
    def _():
        outl_ref[0] = val
        if not combine:
            xcl_ref[0] = xc


def _lru_call(ux_c, ux_l, p, direction, reverse, combine_with=None):
    b, s_len, w = ux_l.shape
    c_len = ux_c.shape[1]
    chunk = c_len
    assert s_len % chunk == 0 and chunk % SUBLANES == 0
    n = s_len // chunk
    hpc = chunk // SUBLANES
    n_hblk = s_len // SUBLANES

    def cidx(s):
        return jnp.clip((n - s) if reverse else (s - 1), 0, n - 1)

    ctx_spec = pl.BlockSpec((1, c_len, w), lambda bi, s: (bi, 0, 0))
    lat_spec = pl.BlockSpec((1, chunk, w), lambda bi, s: (bi, cidx(s), 0))
    gate_consts = (p['wg'][direction], p['bg'][direction], p['lam'][direction])
    combine = combine_with is not None
    if combine:
        consts = gate_consts
        in_specs = [ctx_spec, lat_spec, *[_const_spec(cst.shape) for cst in consts],
                    ctx_spec, lat_spec, ctx_spec, lat_spec]
        args = [ux_c, ux_l, *consts, *combine_with]
        out_specs = (ctx_spec, lat_spec)
        out_shape = (jax.ShapeDtypeStruct(ux_c.shape, BF16), jax.ShapeDtypeStruct(ux_l.shape, BF16))
    else:
        consts = (p['conv_w'], p['conv_b'], *gate_consts)
        in_specs = [
            ctx_spec, lat_spec,
            pl.BlockSpec((1, SUBLANES, w), lambda bi, s: (bi, jnp.maximum(cidx(s) * hpc - 1, 0), 0)),
            pl.BlockSpec((1, SUBLANES, w), lambda bi, s: (bi, jnp.minimum((cidx(s) + 1) * hpc, n_hblk - 1), 0)),
            *[_const_spec(cst.shape) for cst in consts],
        ]
        args = [ux_c, ux_l, ux_l, ux_l, *consts]
        out_specs = (ctx_spec, lat_spec, ctx_spec, lat_spec)
        out_shape = tuple(jax.ShapeDtypeStruct(a.shape, F32) for a in (ux_c, ux_l, ux_c, ux_l))
    scratch = [
        pltpu.VMEM((chunk, w), F32), pltpu.VMEM((chunk, w), F32), pltpu.VMEM((chunk, w), F32),
        pltpu.VMEM((1, w), F32),
    ]
    if not combine:
        scratch.append(pltpu.VMEM((chunk + 2 * SUBLANES, w), F32))
    return pl.pallas_call(
        functools.partial(_lru_kernel, reverse=reverse, n_chunks=n, chunk=chunk, combine=combine),
        grid=(b, n + 1),
        in_specs=in_specs,
        out_specs=out_specs,
        out_shape=out_shape,
        scratch_shapes=scratch,
        compiler_params=_cparams(("arbitrary", "arbitrary")),
        name="lru_fwd" if combine else "lru_bwd",
    )(*args)


def _merge_kernel(x_ref, att_ref, lru_ref, gs_ref, g_ref, woa_ref, wol_ref, wout_ref, o_ref):
    att = lax.dot_general(att_ref[0], woa_ref[...], TN_DIMS, preferred_element_type=F32)
    lru = _dot(lru_ref[0], wol_ref[...])
    gs = gs_ref[0].astype(F32)
    merged = (gs[:, :D_MODEL] * att + gs[:, D_MODEL:] * lru).astype(BF16)
    o_ref[0] = x_ref[0] + g_ref[0] * _dot(merged, wout_ref[...])


def _merge_call(x, att_t, lru, gs, g, p, mod_row, tm):
    b, t, d = x.shape
    consts = (p['w_o_attn'], p['w_o_lru'], p['w_out'])
    return pl.pallas_call(
        _merge_kernel,
        grid=(b, t // tm),
        in_specs=[
            pl.BlockSpec((1, tm, d), lambda bi, i: (bi, i, 0)),
            pl.BlockSpec((1, N_HEADS * V_HEAD, tm), lambda bi, i: (bi, 0, i)),
            pl.BlockSpec((1, tm, LRU_WIDTH), lambda bi, i: (bi, i, 0)),
            pl.BlockSpec((1, tm, 2 * D_MODEL), lambda bi, i: (bi, i, 0)),
            _mod_spec(mod_row),
            *[_const_spec(c.shape) for c in consts],
        ],
        out_specs=pl.BlockSpec((1, tm, d), lambda bi, i: (bi, i, 0)),
        out_shape=jax.ShapeDtypeStruct(x.shape, F32),
        compiler_params=_cparams(("arbitrary", "arbitrary")),
        name="merge",
    )(x, att_t, lru, gs, g, *consts)


def _block_diag_gates(w_r, w_i):
    def bd(w4):
        z = jnp.zeros((4, LRU_BLOCK, 4, LRU_BLOCK), w4.dtype)
        idx = jnp.arange(4)
        z = z.at[idx, :, idx, :].set(w4)
        return z.reshape(4 * LRU_BLOCK, 4 * LRU_BLOCK)
    groups = []
    for j in range(2):
        groups.append(jnp.concatenate([bd(w_r[4 * j:4 * j + 4]), bd(w_i[4 * j:4 * j + 4])], axis=1))
    return jnp.stack(groups).astype(BF16)


def _layer_params(l, w):
    w_in = w['w_in'][l]
    split = Q_LORA + KV_LORA + QK_ROPE
    wa = jnp.pad(w_in[:, :split], ((0, 0), (0, PROJ_A - split))).astype(BF16)
    wb = w_in[:, split:].astype(BF16)
    w_ukv = w['w_ukv'][l].reshape(KV_LORA, N_HEADS, QK_NOPE + V_HEAD)
    wukv_t = jnp.concatenate([
        w_ukv[:, :, :QK_NOPE].reshape(KV_LORA, N_HEADS * QK_NOPE),
        w_ukv[:, :, QK_NOPE:].reshape(KV_LORA, N_HEADS * V_HEAD)], axis=1).T.astype(BF16)
    return {
        'ff1_w_in': w['ff1_w_in'][l].astype(BF16), 'ff1_w_out': w['ff1_w_out'][l].astype(BF16),
        'ff2_w_in': w['ff2_w_in'][l].astype(BF16), 'ff2_w_out': w['ff2_w_out'][l].astype(BF16),
        'wa': wa, 'wb': wb,
        'gqa': w['q_a_norm'][l].reshape(1, Q_LORA), 'gkva': w['kv_a_norm'][l].reshape(1, KV_LORA),
        'wuq_t': w['w_uq'][l].T.astype(BF16), 'wukv_t': wukv_t,
        'gq': (w['q_norm'][l] * (ATTN_SCALE * math.log2(math.e))).reshape(QK_HEAD, 1),
        'gk': w['k_norm'][l].reshape(QK_HEAD, 1),
        'conv_w': w['conv_w'][l], 'conv_b': w['conv_b'][l].reshape(1, LRU_WIDTH),
        'wg': [_block_diag_gates(w['w_rgate'][l, d], w['w_igate'][l, d]) for d in range(2)],
        'bg': [jnp.concatenate([w['b_rgate'][l, d], w['b_igate'][l, d]]).reshape(1, 2 * LRU_WIDTH)
               for d in range(2)],
        'lam': [w['lru_lambda'][l, d].reshape(1, LRU_WIDTH) for d in range(2)],
        'w_o_attn': w['w_o_attn'][l].astype(BF16), 'w_o_lru': w['w_o_lru'][l].astype(BF16),
        'w_out': w['w_out'][l].astype(BF16),
    }


def _rope_tables_t(n_tokens):
    rows = n_tokens // GRID_W
    row = jnp.repeat(jnp.arange(rows, dtype=F32), GRID_W)
    col = jnp.tile(jnp.arange(GRID_W, dtype=F32), rows)
    inv_freq = ROPE_BASE ** (-jnp.arange(ROPE_PAIRS, dtype=F32) / ROPE_PAIRS)
    ang = jnp.concatenate([inv_freq[:, None] * row[None, :], inv_freq[:, None] * col[None, :]], axis=0)
    return jnp.cos(ang), jnp.sin(ang)


def kernel(x, c, ctx, c_ctx, w_ada, b_ada, ff1_w_in, ff1_w_out, ff2_w_in, ff2_w_out, w_in, q_a_norm, w_uq,
           kv_a_norm, w_ukv, q_norm, k_norm, conv_w, conv_b, lru_lambda, w_rgate, b_rgate, w_igate, b_igate,
           w_o_attn, w_o_lru, w_out):
    weights = dict(ff1_w_in=ff1_w_in, ff1_w_out=ff1_w_out, ff2_w_in=ff2_w_in, ff2_w_out=ff2_w_out, w_in=w_in,
                   q_a_norm=q_a_norm, w_uq=w_uq, kv_a_norm=kv_a_norm, w_ukv=w_ukv, q_norm=q_norm, k_norm=k_norm,
                   conv_w=conv_w, conv_b=conv_b, lru_lambda=lru_lambda, w_rgate=w_rgate, b_rgate=b_rgate,
                   w_igate=w_igate, b_igate=b_igate, w_o_attn=w_o_attn, w_o_lru=w_o_lru, w_out=w_out)
    batch, seq, d = x.shape
    c_len = ctx.shape[1]
    depth = w_ada.shape[0]
    assert d == D_MODEL and c_len == KV_TILE and seq % KV_TILE == 0 and batch < SUBLANES
    ctx_row = batch
    tm = 256
    tm_ffn = min(512, seq)
    tq = min(1024, seq)
    assert seq % tq == 0 and seq % tm_ffn == 0

    cond_raw = jnp.zeros((SUBLANES, d), F32).at[:batch].set(c).at[ctx_row].set(c_ctx)
    mods = _ada_call(cond_raw, w_ada, b_ada)
    mods = mods.reshape(depth, SUBLANES, N_MOD, 1, d).transpose(0, 2, 1, 3, 4)

    cos_l, sin_l = _rope_tables_t(seq)
    cos_c = jnp.ones((2 * ROPE_PAIRS, c_len), F32)
    sin_c = jnp.zeros((2 * ROPE_PAIRS, c_len), F32)

    xl, xc = x, ctx
    for l in range(depth):
        p = _layer_params(l, weights)
        m = mods[l]
        last = l == depth - 1
        xl = _ffn_call(xl, m[0], m[1], m[2], p['ff1_w_in'], p['ff1_w_out'], None, tm_ffn)
        xc = _ffn_call(xc, m[0], m[1], m[2], p['ff1_w_in'], p['ff1_w_out'], ctx_row, c_len)
        kv_tokens = seq + c_len
        qt_l, k_all, v_all, ux_l, uy_l, gs_l = _proj_call(xl, m[3], m[4], p, cos_l, sin_l, None, tm, kv_tokens, 0)
        qt_c, k_all, v_all, ux_c, uy_c, gs_c = _proj_call(xc, m[3], m[4], p, cos_c, sin_c, ctx_row, c_len,
                                                          kv_tokens, seq, kv_buffers=(k_all, v_all))
        n_kv_tiles = kv_tokens // KV_TILE
        group = ATTN_GROUP if n_kv_tiles % ATTN_GROUP == 0 else 1
        score_bound = (QK_HEAD * ATTN_SCALE * math.log2(math.e) * 1.01
                       * jnp.max(jnp.abs(q_norm[l])) * jnp.max(jnp.abs(k_norm[l])))
        att_l = lax.cond(
            score_bound <= SCORE_BOUND_LOG2,
            lambda: _attn_bounded_call(qt_l, k_all, v_all, n_kv_tiles, tq, ATTN_TILES_PER_BODY),
            lambda: _attn_call(qt_l, k_all, v_all, 0, n_kv_tiles, tq, group=group, unroll=ATTN_PAIR_UNROLL))
        hb_c, hb_l, xv_c, xv_l = _lru_call(ux_c, ux_l, p, 1, True)
        lru_c, lru_l = _lru_call(xv_c, xv_l, p, 0, False, combine_with=(hb_c, hb_l, uy_c, uy_l))
        xl = _merge_call(xl, att_l, lru_l, gs_l, m[5], p, None, tm_ffn)
        xl = _ffn_call(xl, m[6], m[7], m[8], p['ff2_w_in'], p['ff2_w_out'], None, tm_ffn)
        if not last:
            att_c = _attn_call(qt_c, k_all, v_all, seq // KV_TILE, c_len // KV_TILE, c_len)
            xc = _merge_call(xc, att_c, lru_c, gs_c, m[5], p, ctx_row, c_len)
            xc = _ffn_call(xc, m[6], m[7], m[8], p['ff2_w_in'], p['ff2_w_out'], ctx_row, c_len)
    return xl
```

```python
import functools
import math

import jax
import jax.numpy as jnp
from jax import lax
from jax.experimental import pallas as pl
from jax.experimental.pallas import tpu as pltpu

F32 = jnp.float32
BF16 = jnp.bfloat16

D_MODEL = 1024
N_HEADS = 8
QK_NOPE = 64
QK_ROPE = 32
QK_HEAD = QK_NOPE + QK_ROPE
V_HEAD = 64
Q_LORA = 384
KV_LORA = 256
LRU_WIDTH = 512
LRU_BLOCKS = 8
LRU_BLOCK = LRU_WIDTH // LRU_BLOCKS
CONV_W = 4
RG_C = 8.0
D_FF = 2816
N_MOD = 9
EPS = 1e-6
GRID_W = 64
ROPE_PAIRS = QK_ROPE // 4
ROPE_BASE = 10000.0
ATTN_SCALE = QK_HEAD ** -0.5

LANES = 128
SUBLANES = 8
MXU_DIM = 256
VMEM_LIMIT_BYTES = 60 * 1024 * 1024

HEAD_PAD = LANES
V_AUG = V_HEAD + 16
Q_BLOCK_LANES = MXU_DIM
KV_TILE = MXU_DIM
PROJ_A = 768
PROJ_B = 2 * LRU_WIDTH + 2 * D_MODEL
ADA_TN = 1152
ATTN_BOUNDED_GROUP = 1
ATTN_TILES_PER_BODY = 32
SCORE_BOUND_LOG2 = 60.0
ATTN_GROUP = 1
ATTN_PAIR_UNROLL = 4

NT_DIMS = (((1,), (1,)), ((), ()))
TN_DIMS = (((0,), (0,)), ((), ()))


def _dot(a, b):
    return jnp.dot(a, b, preferred_element_type=F32)


def _rms_rows(x):
    return x * lax.rsqrt(jnp.mean(x * x, axis=-1, keepdims=True) + EPS)


def _cparams(sem, flags=None):
    return pltpu.CompilerParams(dimension_semantics=sem, vmem_limit_bytes=VMEM_LIMIT_BYTES, flags=flags)


def _const_spec(shape):
    nd = len(shape)
    return pl.BlockSpec(shape, lambda *_: (0,) * nd, pipeline_mode=pl.Buffered(1))


def _ada_kernel(c_ref, w_ref, b_ref, o_ref):
    c = c_ref[...]
    cond = c * jax.nn.sigmoid(c)
    o_ref[0] = jnp.dot(cond, w_ref[0], preferred_element_type=F32,
                       precision=lax.Precision.HIGHEST) + b_ref[0]


def _ada_call(cond_raw, w_ada, b_ada):
    depth, d, n = w_ada.shape
    rows = cond_raw.shape[0]
    return pl.pallas_call(
        _ada_kernel,
        grid=(depth, n // ADA_TN),
        in_specs=[
            pl.BlockSpec((rows, d), lambda l, j: (0, 0)),
            pl.BlockSpec((1, d, ADA_TN), lambda l, j: (l, 0, j)),
            pl.BlockSpec((1, 1, ADA_TN), lambda l, j: (l, 0, j)),
        ],
        out_specs=pl.BlockSpec((1, rows, ADA_TN), lambda l, j: (l, 0, j)),
        out_shape=jax.ShapeDtypeStruct((depth, rows, n), F32),
        compiler_params=_cparams(("arbitrary", "arbitrary")),
        name="ada",
    )(cond_raw, w_ada, b_ada.reshape(depth, 1, n))


def _mod_spec(mod_row):
    if mod_row is None:
        return pl.BlockSpec((1, 1, D_MODEL), lambda b, i: (b, 0, 0))
    return pl.BlockSpec((1, 1, D_MODEL), lambda b, i: (mod_row, 0, 0))


def _ffn_kernel(x_ref, sh_ref, sc_ref, g_ref, win_ref, wout_ref, o_ref):
    x = x_ref[0]
    xb = (_rms_rows(x) * (1.0 + sc_ref[0]) + sh_ref[0]).astype(BF16)
    g = _dot(xb, win_ref[:, :D_FF])
    u = _dot(xb, win_ref[:, D_FF:])
    a = (g * jax.nn.sigmoid(g) * u).astype(BF16)
    y = _dot(a, wout_ref[...])
    o_ref[0] = x + (0.5 * g_ref[0]) * y


def _ffn_call(x, sh, sc, g, w_in, w_out, mod_row, tm):
    b, t, d = x.shape
    return pl.pallas_call(
        _ffn_kernel,
        grid=(b, t // tm),
        in_specs=[
            pl.BlockSpec((1, tm, d), lambda bi, i: (bi, i, 0)),
            _mod_spec(mod_row), _mod_spec(mod_row), _mod_spec(mod_row),
            _const_spec(w_in.shape), _const_spec(w_out.shape),
        ],
        out_specs=pl.BlockSpec((1, tm, d), lambda bi, i: (bi, i, 0)),
        out_shape=jax.ShapeDtypeStruct(x.shape, F32),
        compiler_params=_cparams(("arbitrary", "arbitrary")),
        name="ffn",
    )(x, sh, sc, g, w_in, w_out)


def _rope_t(t, cos, sin):
    p = ROPE_PAIRS
    x1r, x2r, x1c, x2c = t[0:p], t[p:2 * p], t[2 * p:3 * p], t[3 * p:4 * p]
    cr, cc = cos[0:p], cos[p:2 * p]
    sr, sc = sin[0:p], sin[p:2 * p]
    return jnp.concatenate(
        [x1r * cr - x2r * sr, x1r * sr + x2r * cr, x1c * cc - x2c * sc, x1c * sc + x2c * cc], axis=0)


def _proj_kernel(x_ref, sh_ref, sc_ref, wa_ref, wb_ref, gqa_ref, gkva_ref, wuq_ref, wukv_ref,
                 gq_ref, gk_ref, cos_ref, sin_ref,
                 qt_ref, k_ref, v_ref, ux_ref, uy_ref, gs_ref):
    tm = x_ref.shape[1]
    x = x_ref[0]
    xb = (_rms_rows(x) * (1.0 + sc_ref[0]) + sh_ref[0]).astype(BF16)
    ha = _dot(xb, wa_ref[...])
    hb = _dot(xb, wb_ref[...])
    ux_ref[0] = hb[:, :LRU_WIDTH]
    uy_ref[0] = hb[:, LRU_WIDTH:2 * LRU_WIDTH]
    gs_ref[0] = jax.nn.sigmoid(hb[:, 2 * LRU_WIDTH:]).astype(BF16)

    qn = (_rms_rows(ha[:, :Q_LORA]) * gqa_ref[...]).astype(BF16)
    kvn = (_rms_rows(ha[:, Q_LORA:Q_LORA + KV_LORA]) * gkva_ref[...]).astype(BF16)
    q_t = lax.dot_general(wuq_ref[...], qn, NT_DIMS, preferred_element_type=F32)
    kv_t = lax.dot_general(wukv_ref[...], kvn, NT_DIMS, preferred_element_type=F32)
    kr_t = ha[:, Q_LORA + KV_LORA:].T[:QK_ROPE]

    cos = cos_ref[...]
    sin = sin_ref[...]
    gq = gq_ref[...]
    gk = gk_ref[...]
    kr_rot = _rope_t(kr_t * gk[QK_NOPE:], cos, sin)
    kr_ss = jnp.sum(kr_t * kr_t, axis=0, keepdims=True)
    zpad = jnp.zeros((HEAD_PAD - QK_HEAD, tm), F32)
    ones_rows = jnp.where(lax.broadcasted_iota(jnp.int32, (V_AUG - V_HEAD, tm), 0) == 0, 1.0, 0.0)
    inv_d = 1.0 / QK_HEAD
    for h in range(N_HEADS):
        q = q_t[QK_HEAD * h:QK_HEAD * (h + 1)]
        rq = lax.rsqrt(jnp.sum(q * q, axis=0, keepdims=True) * inv_d + EPS)
        qs = q * rq * gq
        qh = jnp.concatenate([qs[:QK_NOPE], _rope_t(qs[QK_NOPE:], cos, sin), zpad], axis=0)
        qt_ref[0, HEAD_PAD * h:HEAD_PAD * (h + 1), :] = qh.astype(BF16)

        kn = kv_t[QK_NOPE * h:QK_NOPE * (h + 1)]
        rk = lax.rsqrt((jnp.sum(kn * kn, axis=0, keepdims=True) + kr_ss) * inv_d + EPS)
        kh = jnp.concatenate([kn * gk[:QK_NOPE] * rk, kr_rot * rk, zpad], axis=0)
        k_ref[0, h] = kh.T.astype(BF16)

        v0 = N_HEADS * QK_NOPE + V_HEAD * h
        vt = jnp.concatenate([kv_t[v0:v0 + V_HEAD], ones_rows], axis=0).astype(BF16)
        for c in range(tm // KV_TILE):
            v_ref[0, h, c] = vt[:, KV_TILE * c:KV_TILE * (c + 1)]


def _proj_call(x, sh, sc, p, cos_t, sin_t, mod_row, tm, kv_tokens, kv_offset, kv_buffers=None):
    b, t, d = x.shape
    off_blk = kv_offset // tm
    out_shape = (
        jax.ShapeDtypeStruct((b, N_HEADS * HEAD_PAD, t), BF16),
        jax.ShapeDtypeStruct((b, N_HEADS, kv_tokens, HEAD_PAD), BF16),
        jax.ShapeDtypeStruct((b, N_HEADS, kv_tokens // KV_TILE, V_AUG, KV_TILE), BF16),
        jax.ShapeDtypeStruct((b, t, LRU_WIDTH), F32),
        jax.ShapeDtypeStruct((b, t, LRU_WIDTH), F32),
        jax.ShapeDtypeStruct((b, t, 2 * D_MODEL), BF16),
    )
    out_specs = (
        pl.BlockSpec((1, N_HEADS * HEAD_PAD, tm), lambda bi, i: (bi, 0, i)),
        pl.BlockSpec((1, N_HEADS, tm, HEAD_PAD), lambda bi, i: (bi, 0, off_blk + i, 0)),
        pl.BlockSpec((1, N_HEADS, tm // KV_TILE, V_AUG, KV_TILE), lambda bi, i: (bi, 0, off_blk + i, 0, 0)),
        pl.BlockSpec((1, tm, LRU_WIDTH), lambda bi, i: (bi, i, 0)),
        pl.BlockSpec((1, tm, LRU_WIDTH), lambda bi, i: (bi, i, 0)),
        pl.BlockSpec((1, tm, 2 * D_MODEL), lambda bi, i: (bi, i, 0)),
    )
    consts = (p['wa'], p['wb'], p['gqa'], p['gkva'], p['wuq_t'], p['wukv_t'], p['gq'], p['gk'])
    in_specs = [
        pl.BlockSpec((1, tm, d), lambda bi, i: (bi, i, 0)),
        _mod_spec(mod_row), _mod_spec(mod_row),
        *[_const_spec(c.shape) for c in consts],
        pl.BlockSpec((2 * ROPE_PAIRS, tm), lambda bi, i: (0, i)),
        pl.BlockSpec((2 * ROPE_PAIRS, tm), lambda bi, i: (0, i)),
    ]
    args = [x, sh, sc, *consts, cos_t, sin_t]
    aliases = {}
    if kv_buffers is not None:
        aliases = {len(args): 1, len(args) + 1: 2}
        in_specs += [pl.BlockSpec(memory_space=pl.ANY), pl.BlockSpec(memory_space=pl.ANY)]
        args += list(kv_buffers)
    n_in = len(args)

    def body(*refs):
        _proj_kernel(*refs[:n_in - len(aliases)], *refs[n_in:])

    return pl.pallas_call(
        body,
        grid=(b, t // tm),
        in_specs=in_specs,
        out_specs=out_specs,
        out_shape=out_shape,
        input_output_aliases=aliases,
        compiler_params=_cparams(("arbitrary", "arbitrary")),
        name="proj",
    )(*args)


def _col_max(s):
    rows = s.shape[0]
    while rows > 4 * SUBLANES and rows % 2 == 0:
        rows //= 2
        s = jnp.maximum(s[:rows], s[rows:])
    return jnp.max(s, axis=0, keepdims=True)


def _attn_kernel(q_ref, k_ref, v_ref, o_ref, s_scr, mt_scr, *, n_steps, group, unroll):
    tq = q_ref.shape[2]
    n_blocks = tq // Q_BLOCK_LANES
    step_keys = group * KV_TILE
    q_blocks = [q_ref[0, :, Q_BLOCK_LANES * i:Q_BLOCK_LANES * (i + 1)] for i in range(n_blocks)]

    def k_step(j):
        off = j * step_keys if isinstance(j, int) else pl.multiple_of(j * step_keys, step_keys)
        return k_ref[0, 0, pl.ds(off, step_keys), :]

    def v_step(j):
        tiles = [v_ref[0, 0, group * j + t] for t in range(group)]
        return tiles[0] if group == 1 else jnp.concatenate(tiles, axis=1)

    def scores(j, slot):
        kt = k_step(j)
        for b, q_b in enumerate(q_blocks):
            s = _dot(kt, q_b)
            s_scr[slot, b] = s
            mt_scr[slot, b] = _col_max(s)

    def update(slot, j, state):
        vt = v_step(j)
        out = []
        for b, (m, acc) in enumerate(state):
            m_new = jnp.maximum(m, mt_scr[slot, b])
            p = jnp.exp2(s_scr[slot, b] - m_new).astype(BF16)
            alpha = jnp.exp2(m - m_new)
            out.append((m_new, alpha * acc + _dot(vt, p)))
        return tuple(out)

    n_pairs = (n_steps - 1) // 2
    state = tuple((jnp.full((1, Q_BLOCK_LANES), -jnp.inf, F32), jnp.zeros((V_AUG, Q_BLOCK_LANES), F32))
                  for _ in range(n_blocks))
    scores(0, 0)
    if n_pairs:
        def body(g, st):
            j = 2 * g
            scores(j + 1, 1)
            st = update(0, j, st)
            scores(j + 2, 0)
            return update(1, j + 1, st)
        state = lax.fori_loop(0, n_pairs, body, state, unroll=math.gcd(unroll, n_pairs))
    for j in range(2 * n_pairs, n_steps):
        slot = j % 2
        if j + 1 < n_steps:
            scores(j + 1, 1 - slot)
        state = update(slot, j, state)
    for i, (_, acc) in enumerate(state):
        o_ref[0, :, Q_BLOCK_LANES * i:Q_BLOCK_LANES * (i + 1)] = (
            acc[:V_HEAD] / acc[V_HEAD:V_HEAD + 1]).astype(BF16)


def _attn_call(q_t, k, v, first_tile, n_tiles, tq, group=1, unroll=1):
    b, hp, t = q_t.shape
    tk = n_tiles * KV_TILE
    assert first_tile % n_tiles == 0 and n_tiles % group == 0
    blk = first_tile // n_tiles
    n_blocks = tq // Q_BLOCK_LANES
    return pl.pallas_call(
        functools.partial(_attn_kernel, n_steps=n_tiles // group, group=group, unroll=unroll),
        grid=(b, N_HEADS, t // tq),
        in_specs=[
            pl.BlockSpec((1, HEAD_PAD, tq), lambda bi, h, i: (bi, h, i)),
            pl.BlockSpec((1, 1, tk, HEAD_PAD), lambda bi, h, i: (bi, h, blk, 0)),
            pl.BlockSpec((1, 1, n_tiles, V_AUG, KV_TILE), lambda bi, h, i: (bi, h, blk, 0, 0)),
        ],
        out_specs=pl.BlockSpec((1, V_HEAD, tq), lambda bi, h, i: (bi, h, i)),
        out_shape=jax.ShapeDtypeStruct((b, N_HEADS * V_HEAD, t), BF16),
        scratch_shapes=[
            pltpu.VMEM((2, n_blocks, group * KV_TILE, Q_BLOCK_LANES), F32),
            pltpu.VMEM((2, n_blocks, 1, Q_BLOCK_LANES), F32),
        ],
        compiler_params=_cparams(("arbitrary", "arbitrary", "arbitrary")),
        name="attn",
    )(q_t, k, v)


def _row_group_sum(p):
    rows = p.shape[0]
    while rows % (2 * SUBLANES) == 0:
        rows //= 2
        p = p[:rows] + p[rows:]
    return p


def _attn_bounded_kernel(q_ref, k_ref, v_ref, o_ref, s_scr, *, n_steps, group, tiles_per_body):
    tq = q_ref.shape[2]
    n_blocks = tq // Q_BLOCK_LANES
    step_keys = group * KV_TILE
    q_blocks = [q_ref[0, :, Q_BLOCK_LANES * i:Q_BLOCK_LANES * (i + 1)] for i in range(n_blocks)]

    def k_step(j):
        off = j * step_keys if isinstance(j, int) else pl.multiple_of(j * step_keys, step_keys)
        return k_ref[0, 0, pl.ds(off, step_keys), :]

    def scores(j):
        s_all = _dot(k_step(j), q_ref[0])
        return [s_all[:, Q_BLOCK_LANES * i:Q_BLOCK_LANES * (i + 1)] for i in range(n_blocks)]

    def accumulate(s_blocks, j, accs):
        tiles = [v_ref[0, 0, group * j + t][:V_HEAD] for t in range(group)]
        vt = tiles[0] if group == 1 else jnp.concatenate(tiles, axis=1)
        out = []
        for s, (num, den) in zip(s_blocks, accs):
            p = jnp.exp2(s)
            out.append((num + _dot(vt, p.astype(BF16)), den + _row_group_sum(p)))
        return tuple(out)

    den_rows = step_keys
    while den_rows % (2 * SUBLANES) == 0:
        den_rows //= 2
    accs = tuple((jnp.zeros((V_HEAD, Q_BLOCK_LANES), F32), jnp.zeros((den_rows, Q_BLOCK_LANES), F32))
                 for _ in range(n_blocks))
    n_bodies = (n_steps - 1) // tiles_per_body
    for b, s in enumerate(scores(0)):
        s_scr[b] = s
    if n_bodies:
        def body(g, accs):
            j0 = g * tiles_per_body
            s_cur = [s_scr[b] for b in range(n_blocks)]
            for t in range(tiles_per_body):
                s_next = scores(j0 + t + 1)
                accs = accumulate(s_cur, j0 + t, accs)
                s_cur = s_next
            for b, s in enumerate(s_cur):
                s_scr[b] = s
            return accs
        accs = lax.fori_loop(0, n_bodies, body, accs)
    s_cur = [s_scr[b] for b in range(n_blocks)]
    for j in range(n_bodies * tiles_per_body, n_steps):
        s_next = scores(j + 1) if j + 1 < n_steps else None
        accs = accumulate(s_cur, j, accs)
        s_cur = s_next
    for i, (num, den) in enumerate(accs):
        o_ref[0, :, Q_BLOCK_LANES * i:Q_BLOCK_LANES * (i + 1)] = (
            num / jnp.sum(den, axis=0, keepdims=True)).astype(BF16)


def _attn_bounded_call(q_t, k, v, n_tiles, tq, group, tiles_per_body):
    b, hp, t = q_t.shape
    n_blocks = tq // Q_BLOCK_LANES
    assert n_tiles % group == 0
    return pl.pallas_call(
        functools.partial(_attn_bounded_kernel, n_steps=n_tiles // group, group=group,
                          tiles_per_body=tiles_per_body),
        grid=(b, N_HEADS, t // tq),
        in_specs=[
            pl.BlockSpec((1, HEAD_PAD, tq), lambda bi, h, i: (bi, h, i)),
            pl.BlockSpec((1, 1, n_tiles * KV_TILE, HEAD_PAD), lambda bi, h, i: (bi, h, 0, 0)),
            pl.BlockSpec((1, 1, n_tiles, V_AUG, KV_TILE), lambda bi, h, i: (bi, h, 0, 0, 0)),
        ],
        out_specs=pl.BlockSpec((1, V_HEAD, tq), lambda bi, h, i: (bi, h, i)),
        out_shape=jax.ShapeDtypeStruct((b, N_HEADS * V_HEAD, t), BF16),
        scratch_shapes=[pltpu.VMEM((n_blocks, group * KV_TILE, Q_BLOCK_LANES), F32)],
        compiler_params=_cparams(("arbitrary", "arbitrary", "arbitrary")),
        name="attn_bounded",
    )(q_t, k, v)


def _gelu_tanh(x):
    return 0.5 * x * (1.0 + jnp.tanh(math.sqrt(2.0 / math.pi) * (x + 0.044715 * (x * x * x))))


def _short_conv(cur, prev8, next8, xbuf, cw, cb):
    chunk = cur.shape[0]
    halo = SUBLANES
    xbuf[0:halo] = prev8
    xbuf[halo:halo + chunk] = cur
    xbuf[halo + chunk:2 * halo + chunk] = next8
    xc = cb
    for k in range(CONV_W):
        o = halo - 2 + k
        xc = xc + cw[k:k + 1] * xbuf[o:o + chunk]
    return xc


def _lru_kernel(*refs, reverse, n_chunks, chunk, combine):
    width = LRU_WIDTH
    s = pl.program_id(1)
    is_ctx = s == 0
    c = jnp.clip((n_chunks - s) if reverse else (s - 1), 0, n_chunks - 1)
    if combine:
        (xcc_ref, xcl_ref, wg_ref, bg_ref, lam_ref, hbc_ref, hbl_ref, uyc_ref, uy_ref,
         outc_ref, outl_ref, a_s, b_s, h_s, carry_s) = refs
        xc = jnp.where(is_ctx, xcc_ref[0], xcl_ref[0])
    else:
        (uxc_ref, ux_ref, uxp_ref, uxn_ref, cw_ref, cb_ref, wg_ref, bg_ref, lam_ref,
         outc_ref, outl_ref, xcc_ref, xcl_ref, a_s, b_s, h_s, carry_s, xbuf) = refs
        zero_h = jnp.zeros((SUBLANES, width), F32)
        prev8 = jnp.where(is_ctx | (c == 0), zero_h, uxp_ref[0])
        next8 = jnp.where(is_ctx | (c == n_chunks - 1), zero_h, uxn_ref[0])
        cur = jnp.where(is_ctx, uxc_ref[0], ux_ref[0])
        xc = _short_conv(cur, prev8, next8, xbuf, cw_ref[...], cb_ref[...])

    xcb = xc.astype(BF16)
    half = width // 2
    d0 = _dot(xcb[:, :half], wg_ref[0])
    d1 = _dot(xcb[:, half:], wg_ref[1])
    bg = bg_ref[...]
    r = jax.nn.sigmoid(jnp.concatenate([d0[:, :half], d1[:, :half]], axis=1) + bg[:, :width])
    i = jax.nn.sigmoid(jnp.concatenate([d0[:, half:], d1[:, half:]], axis=1) + bg[:, width:])
    nl = -lam_ref[...]
    softplus = jnp.maximum(nl, 0.0) + jnp.log(1.0 + jnp.exp(-jnp.abs(nl)))
    a = jnp.exp(-RG_C * r * softplus)
    a_s[...] = a
    v = 1.0 - a * a
    b_s[...] = jnp.where(v > 0.0, v * lax.rsqrt(v), 0.0) * (i * xc)

    row = lax.broadcasted_iota(jnp.int32, (SUBLANES, width), 0)
    n_groups = chunk // SUBLANES

    def body(g, carry):
        gg = (n_groups - 1 - g) if reverse else g
        off = pl.multiple_of(gg * SUBLANES, SUBLANES)
        av = a_s[pl.ds(off, SUBLANES), :]
        bv = b_s[pl.ds(off, SUBLANES), :]
        for k in (1, 2, 4):
            shift = (SUBLANES - k) if reverse else k
            valid = (row < SUBLANES - k) if reverse else (row >= k)
            a_sh = jnp.where(valid, pltpu.roll(av, shift, 0), 1.0)
            b_sh = jnp.where(valid, pltpu.roll(bv, shift, 0), 0.0)
            bv = av * b_sh + bv
            av = av * a_sh
        hblk = av * carry + bv
        h_s[pl.ds(off, SUBLANES), :] = hblk
        return hblk[0:1] if reverse else hblk[SUBLANES - 1:SUBLANES]

    carry0 = jnp.where(is_ctx, jnp.zeros((1, width), F32), carry_s[...])
    carry_s[...] = lax.fori_loop(0, n_groups, body, carry0)
    h = h_s[...]

    if combine:
        hb = jnp.where(is_ctx, hbc_ref[0], hbl_ref[0])
        uy = jnp.where(is_ctx, uyc_ref[0], uy_ref[0])
        val = ((h + hb) * _gelu_tanh(uy)).astype(BF16)
    else:
        val = h

    @pl.when(is_ctx)
    def _():
        outc_ref[0] = val
        if not combine:
            xcc_ref[0] = xc

    @pl.when(jnp.logical_not(is_ctx))
    def _():
        outl_ref[0] = val
        if not combine:
            xcl_ref[0] = xc


def _lru_call(ux_c, ux_l, p, direction, reverse, combine_with=None):
    b, s_len, w = ux_l.shape
    c_len = ux_c.shape[1]
    chunk = c_len
    assert s_len % chunk == 0 and chunk % SUBLANES == 0
    n = s_len // chunk
    hpc = chunk // SUBLANES
    n_hblk = s_len // SUBLANES

    def cidx(s):
        return jnp.clip((n - s) if reverse else (s - 1), 0, n - 1)

    ctx_spec = pl.BlockSpec((1, c_len, w), lambda bi, s: (bi, 0, 0))
    lat_spec = pl.BlockSpec((1, chunk, w), lambda bi, s: (bi, cidx(s), 0))
    gate_consts = (p['wg'][direction], p['bg'][direction], p['lam'][direction])
    combine = combine_with is not None
    if combine:
        consts = gate_consts
        in_specs = [ctx_spec, lat_spec, *[_const_spec(cst.shape) for cst in consts],
                    ctx_spec, lat_spec, ctx_spec, lat_spec]
        args = [ux_c, ux_l, *consts, *combine_with]
        out_specs = (ctx_spec, lat_spec)
        out_shape = (jax.ShapeDtypeStruct(ux_c.shape, BF16), jax.ShapeDtypeStruct(ux_l.shape, BF16))
    else:
        consts = (p['conv_w'], p['conv_b'], *gate_consts)
        in_specs = [
            ctx_spec, lat_spec,
            pl.BlockSpec((1, SUBLANES, w), lambda bi, s: (bi, jnp.maximum(cidx(s) * hpc - 1, 0), 0)),
            pl.BlockSpec((1, SUBLANES, w), lambda bi, s: (bi, jnp.minimum((cidx(s) + 1) * hpc, n_hblk - 1), 0)),
            *[_const_spec(cst.shape) for cst in consts],
        ]
        args = [ux_c, ux_l, ux_l, ux_l, *consts]
        out_specs = (ctx_spec, lat_spec, ctx_spec, lat_spec)
        out_shape = tuple(jax.ShapeDtypeStruct(a.shape, F32) for a in (ux_c, ux_l, ux_c, ux_l))
    scratch = [
        pltpu.VMEM((chunk, w), F32), pltpu.VMEM((chunk, w), F32), pltpu.VMEM((chunk, w), F32),
        pltpu.VMEM((1, w), F32),
    ]
    if not combine:
        scratch.append(pltpu.VMEM((chunk + 2 * SUBLANES, w), F32))
    return pl.pallas_call(
        functools.partial(_lru_kernel, reverse=reverse, n_chunks=n, chunk=chunk, combine=combine),
        grid=(b, n + 1),
        in_specs=in_specs,
        out_specs=out_specs,
        out_shape=out_shape,
        scratch_shapes=scratch,
        compiler_params=_cparams(("arbitrary", "arbitrary")),
        name="lru_fwd" if combine else "lru_bwd",
    )(*args)


def _merge_kernel(x_ref, att_ref, lru_ref, gs_ref, g_ref, woa_ref, wol_ref, wout_ref, o_ref):
    att = lax.dot_general(att_ref[0], woa_ref[...], TN_DIMS, preferred_element_type=F32)
    lru = _dot(lru_ref[0], wol_ref[...])
    gs = gs_ref[0].astype(F32)
    merged = (gs[:, :D_MODEL] * att + gs[:, D_MODEL:] * lru).astype(BF16)
    o_ref[0] = x_ref[0] + g_ref[0] * _dot(merged, wout_ref[...])


def _merge_call(x, att_t, lru, gs, g, p, mod_row, tm):
    b, t, d = x.shape
    consts = (p['w_o_attn'], p['w_o_lru'], p['w_out'])
    return pl.pallas_call(
        _merge_kernel,
        grid=(b, t // tm),
        in_specs=[
            pl.BlockSpec((1, tm, d), lambda bi, i: (bi, i, 0)),
            pl.BlockSpec((1, N_HEADS * V_HEAD, tm), lambda bi, i: (bi, 0, i)),
            pl.BlockSpec((1, tm, LRU_WIDTH), lambda bi, i: (bi, i, 0)),
            pl.BlockSpec((1, tm, 2 * D_MODEL), lambda bi, i: (bi, i, 0)),
            _mod_spec(mod_row),
            *[_const_spec(c.shape) for c in consts],
        ],
        out_specs=pl.BlockSpec((1, tm, d), lambda bi, i: (bi, i, 0)),
        out_shape=jax.ShapeDtypeStruct(x.shape, F32),
        compiler_params=_cparams(("arbitrary", "arbitrary")),
        name="merge",
    )(x, att_t, lru, gs, g, *consts)


def _block_diag_gates(w_r, w_i):
    def bd(w4):
        z = jnp.zeros((4, LRU_BLOCK, 4, LRU_BLOCK), w4.dtype)
        idx = jnp.arange(4)
        z = z.at[idx, :, idx, :].set(w4)
        return z.reshape(4 * LRU_BLOCK, 4 * LRU_BLOCK)
    groups = []
    for j in range(2):
        groups.append(jnp.concatenate([bd(w_r[4 * j:4 * j + 4]), bd(w_i[4 * j:4 * j + 4])], axis=1))
    return jnp.stack(groups).astype(BF16)


def _layer_params(l, w):
    w_in = w['w_in'][l]
    split = Q_LORA + KV_LORA + QK_ROPE
    wa = jnp.pad(w_in[:, :split], ((0, 0), (0, PROJ_A - split))).astype(BF16)
    wb = w_in[:, split:].astype(BF16)
    w_ukv = w['w_ukv'][l].reshape(KV_LORA, N_HEADS, QK_NOPE + V_HEAD)
    wukv_t = jnp.concatenate([
        w_ukv[:, :, :QK_NOPE].reshape(KV_LORA, N_HEADS * QK_NOPE),
        w_ukv[:, :, QK_NOPE:].reshape(KV_LORA, N_HEADS * V_HEAD)], axis=1).T.astype(BF16)
    return {
        'ff1_w_in': w['ff1_w_in'][l].astype(BF16), 'ff1_w_out': w['ff1_w_out'][l].astype(BF16),
        'ff2_w_in': w['ff2_w_in'][l].astype(BF16), 'ff2_w_out': w['ff2_w_out'][l].astype(BF16),
        'wa': wa, 'wb': wb,
        'gqa': w['q_a_norm'][l].reshape(1, Q_LORA), 'gkva': w['kv_a_norm'][l].reshape(1, KV_LORA),
        'wuq_t': w['w_uq'][l].T.astype(BF16), 'wukv_t': wukv_t,
        'gq': (w['q_norm'][l] * (ATTN_SCALE * math.log2(math.e))).reshape(QK_HEAD, 1),
        'gk': w['k_norm'][l].reshape(QK_HEAD, 1),
        'conv_w': w['conv_w'][l], 'conv_b': w['conv_b'][l].reshape(1, LRU_WIDTH),
        'wg': [_block_diag_gates(w['w_rgate'][l, d], w['w_igate'][l, d]) for d in range(2)],
        'bg': [jnp.concatenate([w['b_rgate'][l, d], w['b_igate'][l, d]]).reshape(1, 2 * LRU_WIDTH)
               for d in range(2)],
        'lam': [w['lru_lambda'][l, d].reshape(1, LRU_WIDTH) for d in range(2)],
        'w_o_attn': w['w_o_attn'][l].astype(BF16), 'w_o_lru': w['w_o_lru'][l].astype(BF16),
        'w_out': w['w_out'][l].astype(BF16),
    }


def _rope_tables_t(n_tokens):
    rows = n_tokens // GRID_W
    row = jnp.repeat(jnp.arange(rows, dtype=F32), GRID_W)
    col = jnp.tile(jnp.arange(GRID_W, dtype=F32), rows)
    inv_freq = ROPE_BASE ** (-jnp.arange(ROPE_PAIRS, dtype=F32) / ROPE_PAIRS)
    ang = jnp.concatenate([inv_freq[:, None] * row[None, :], inv_freq[:, None] * col[None, :]], axis=0)
    return jnp.cos(ang), jnp.sin(ang)


def kernel(x, c, ctx, c_ctx, w_ada, b_ada, ff1_w_in, ff1_w_out, ff2_w_in, ff2_w_out, w_in, q_a_norm, w_uq,
           kv_a_norm, w_ukv, q_norm, k_norm, conv_w, conv_b, lru_lambda, w_rgate, b_rgate, w_igate, b_igate,
           w_o_attn, w_o_lru, w_out):
    weights = dict(ff1_w_in=ff1_w_in, ff1_w_out=ff1_w_out, ff2_w_in=ff2_w_in, ff2_w_out=ff2_w_out, w_in=w_in,
                   q_a_norm=q_a_norm, w_uq=w_uq, kv_a_norm=kv_a_norm, w_ukv=w_ukv, q_norm=q_norm, k_norm=k_norm,
                   conv_w=conv_w, conv_b=conv_b, lru_lambda=lru_lambda, w_rgate=w_rgate, b_rgate=b_rgate,
                   w_igate=w_igate, b_igate=b_igate, w_o_attn=w_o_attn, w_o_lru=w_o_lru, w_out=w_out)
    batch, seq, d = x.shape
    c_len = ctx.shape[1]
    depth = w_ada.shape[0]
    assert d == D_MODEL and c_len == KV_TILE and seq % KV_TILE == 0 and batch < SUBLANES
    ctx_row = batch
    tm = 256
    tm_ffn = min(512, seq)
    tq = min(1024, seq)
    assert seq % tq == 0 and seq % tm_ffn == 0

    cond_raw = jnp.zeros((SUBLANES, d), F32).at[:batch].set(c).at[ctx_row].set(c_ctx)
    mods = _ada_call(cond_raw, w_ada, b_ada)
    mods = mods.reshape(depth, SUBLANES, N_MOD, 1, d).transpose(0, 2, 1, 3, 4)

    cos_l, sin_l = _rope_tables_t(seq)
    cos_c = jnp.ones((2 * ROPE_PAIRS, c_len), F32)
    sin_c = jnp.zeros((2 * ROPE_PAIRS, c_len), F32)

    xl, xc = x, ctx
    for l in range(depth):
        p = _layer_params(l, weights)
        m = mods[l]
        last = l == depth - 1
        xl = _ffn_call(xl, m[0], m[1], m[2], p['ff1_w_in'], p['ff1_w_out'], None, tm_ffn)
        xc = _ffn_call(xc, m[0], m[1], m[2], p['ff1_w_in'], p['ff1_w_out'], ctx_row, c_len)
        kv_tokens = seq + c_len
        qt_l, k_all, v_all, ux_l, uy_l, gs_l = _proj_call(xl, m[3], m[4], p, cos_l, sin_l, None, tm, kv_tokens, 0)
        qt_c, k_all, v_all, ux_c, uy_c, gs_c = _proj_call(xc, m[3], m[4], p, cos_c, sin_c, ctx_row, c_len,
                                                          kv_tokens, seq, kv_buffers=(k_all, v_all))
        n_kv_tiles = kv_tokens // KV_TILE
        group = ATTN_GROUP if n_kv_tiles % ATTN_GROUP == 0 else 1
        score_bound = (QK_HEAD * ATTN_SCALE * math.log2(math.e) * 1.01
                       * jnp.max(jnp.abs(q_norm[l])) * jnp.max(jnp.abs(k_norm[l])))
        att_l = lax.cond(
            score_bound <= SCORE_BOUND_LOG2,
            lambda: _attn_bounded_call(qt_l, k_all, v_all, n_kv_tiles, tq,
                                       ATTN_BOUNDED_GROUP if n_kv_tiles % ATTN_BOUNDED_GROUP == 0 else 1,
                                       ATTN_TILES_PER_BODY),
            lambda: _attn_call(qt_l, k_all, v_all, 0, n_kv_tiles, tq, group=group, unroll=ATTN_PAIR_UNROLL))
        hb_c, hb_l, xv_c, xv_l = _lru_call(ux_c, ux_l, p, 1, True)
        lru_c, lru_l = _lru_call(xv_c, xv_l, p, 0, False, combine_with=(hb_c, hb_l, uy_c, uy_l))
        xl = _merge_call(xl, att_l, lru_l, gs_l, m[5], p, None, tm_ffn)
        xl = _ffn_call(xl, m[6], m[7], m[8], p['ff2_w_in'], p['ff2_w_out'], None, tm_ffn)
        if not last:
            att_c = _attn_call(qt_c, k_all, v_all, seq // KV_TILE, c_len // KV_TILE, c_len)
            xc = _merge_call(xc, att_c, lru_c, gs_c, m[5], p, ctx_row, c_len)
            xc = _ffn_call(xc, m[6], m[7], m[8], p['ff2_w_in'], p['ff2_w_out'], ctx_row, c_len)
    return xl
```

```python
import functools
import math

import jax
import jax.numpy as jnp
from jax import lax
from jax.experimental import pallas as pl
from jax.experimental.pallas import tpu as pltpu

F32 = jnp.float32
BF16 = jnp.bfloat16

D_MODEL = 1024
N_HEADS = 8
QK_NOPE = 64
QK_ROPE = 32
QK_HEAD = QK_NOPE + QK_ROPE
V_HEAD = 64
Q_LORA = 384
KV_LORA = 256
LRU_WIDTH = 512
LRU_BLOCKS = 8
LRU_BLOCK = LRU_WIDTH // LRU_BLOCKS
CONV_W = 4
RG_C = 8.0
D_FF = 2816
N_MOD = 9
EPS = 1e-6
GRID_W = 64
ROPE_PAIRS = QK_ROPE // 4
ROPE_BASE = 10000.0
ATTN_SCALE = QK_HEAD ** -0.5

LANES = 128
SUBLANES = 8
MXU_DIM = 256
VMEM_LIMIT_BYTES = 60 * 1024 * 1024

HEAD_PAD = LANES
V_AUG = V_HEAD + 16
Q_BLOCK_LANES = MXU_DIM
KV_TILE = MXU_DIM
PROJ_A = 768
PROJ_B = 2 * LRU_WIDTH + 2 * D_MODEL
ADA_TN = 1152
ATTN_BOUNDED_GROUP = 1
ATTN_TILES_PER_BODY = 32
SCORE_BOUND_LOG2 = 60.0
ATTN_GROUP = 1
ATTN_PAIR_UNROLL = 4

NT_DIMS = (((1,), (1,)), ((), ()))
TN_DIMS = (((0,), (0,)), ((), ()))


def _dot(a, b):
    return jnp.dot(a, b, preferred_element_type=F32)


def _rms_rows(x):
    return x * lax.rsqrt(jnp.mean(x * x, axis=-1, keepdims=True) + EPS)


def _cparams(sem, flags=None):
    return pltpu.CompilerParams(dimension_semantics=sem, vmem_limit_bytes=VMEM_LIMIT_BYTES, flags=flags)


def _const_spec(shape):
    nd = len(shape)
    return pl.BlockSpec(shape, lambda *_: (0,) * nd, pipeline_mode=pl.Buffered(1))


def _ada_kernel(c_ref, w_ref, b_ref, o_ref):
    c = c_ref[...]
    cond = c * jax.nn.sigmoid(c)
    o_ref[0] = jnp.dot(cond, w_ref[0], preferred_element_type=F32,
                       precision=lax.Precision.HIGHEST) + b_ref[0]


def _ada_call(cond_raw, w_ada, b_ada):
    depth, d, n = w_ada.shape
    rows = cond_raw.shape[0]
    return pl.pallas_call(
        _ada_kernel,
        grid=(depth, n // ADA_TN),
        in_specs=[
            pl.BlockSpec((rows, d), lambda l, j: (0, 0)),
            pl.BlockSpec((1, d, ADA_TN), lambda l, j: (l, 0, j)),
            pl.BlockSpec((1, 1, ADA_TN), lambda l, j: (l, 0, j)),
        ],
        out_specs=pl.BlockSpec((1, rows, ADA_TN), lambda l, j: (l, 0, j)),
        out_shape=jax.ShapeDtypeStruct((depth, rows, n), F32),
        compiler_params=_cparams(("arbitrary", "arbitrary")),
        name="ada",
    )(cond_raw, w_ada, b_ada.reshape(depth, 1, n))


def _mod_spec(mod_row):
    if mod_row is None:
        return pl.BlockSpec((1, 1, D_MODEL), lambda b, i: (b, 0, 0))
    return pl.BlockSpec((1, 1, D_MODEL), lambda b, i: (mod_row, 0, 0))


def _ffn_kernel(x_ref, sh_ref, sc_ref, g_ref, win_ref, wout_ref, o_ref):
    x = x_ref[0]
    xb = (_rms_rows(x) * (1.0 + sc_ref[0]) + sh_ref[0]).astype(BF16)
    g = _dot(xb, win_ref[:, :D_FF])
    u = _dot(xb, win_ref[:, D_FF:])
    a = (g * jax.nn.sigmoid(g) * u).astype(BF16)
    y = _dot(a, wout_ref[...])
    o_ref[0] = x + (0.5 * g_ref[0]) * y


def _ffn_call(x, sh, sc, g, w_in, w_out, mod_row, tm):
    b, t, d = x.shape
    return pl.pallas_call(
        _ffn_kernel,
        grid=(b, t // tm),
        in_specs=[
            pl.BlockSpec((1, tm, d), lambda bi, i: (bi, i, 0)),
            _mod_spec(mod_row), _mod_spec(mod_row), _mod_spec(mod_row),
            _const_spec(w_in.shape), _const_spec(w_out.shape),
        ],
        out_specs=pl.BlockSpec((1, tm, d), lambda bi, i: (bi, i, 0)),
        out_shape=jax.ShapeDtypeStruct(x.shape, F32),
        compiler_params=_cparams(("arbitrary", "arbitrary")),
        name="ffn",
    )(x, sh, sc, g, w_in, w_out)


def _rope_t(t, cos, sin):
    p = ROPE_PAIRS
    x1r, x2r, x1c, x2c = t[0:p], t[p:2 * p], t[2 * p:3 * p], t[3 * p:4 * p]
    cr, cc = cos[0:p], cos[p:2 * p]
    sr, sc = sin[0:p], sin[p:2 * p]
    return jnp.concatenate(
        [x1r * cr - x2r * sr, x1r * sr + x2r * cr, x1c * cc - x2c * sc, x1c * sc + x2c * cc], axis=0)


def _proj_kernel(x_ref, sh_ref, sc_ref, wa_ref, wb_ref, gqa_ref, gkva_ref, wuq_ref, wukv_ref,
                 gq_ref, gk_ref, cos_ref, sin_ref,
                 qt_ref, k_ref, v_ref, ux_ref, uy_ref, gs_ref):
    tm = x_ref.shape[1]
    x = x_ref[0]
    xb = (_rms_rows(x) * (1.0 + sc_ref[0]) + sh_ref[0]).astype(BF16)
    ha = _dot(xb, wa_ref[...])
    hb = _dot(xb, wb_ref[...])
    ux_ref[0] = hb[:, :LRU_WIDTH]
    uy_ref[0] = hb[:, LRU_WIDTH:2 * LRU_WIDTH]
    gs_ref[0] = jax.nn.sigmoid(hb[:, 2 * LRU_WIDTH:]).astype(BF16)

    qn = (_rms_rows(ha[:, :Q_LORA]) * gqa_ref[...]).astype(BF16)
    kvn = (_rms_rows(ha[:, Q_LORA:Q_LORA + KV_LORA]) * gkva_ref[...]).astype(BF16)
    q_t = lax.dot_general(wuq_ref[...], qn, NT_DIMS, preferred_element_type=F32)
    kv_t = lax.dot_general(wukv_ref[...], kvn, NT_DIMS, preferred_element_type=F32)
    kr_t = ha[:, Q_LORA + KV_LORA:].T[:QK_ROPE]

    cos = cos_ref[...]
    sin = sin_ref[...]
    gq = gq_ref[...]
    gk = gk_ref[...]
    kr_rot = _rope_t(kr_t * gk[QK_NOPE:], cos, sin)
    kr_ss = jnp.sum(kr_t * kr_t, axis=0, keepdims=True)
    zpad = jnp.zeros((HEAD_PAD - QK_HEAD, tm), F32)
    ones_rows = jnp.where(lax.broadcasted_iota(jnp.int32, (V_AUG - V_HEAD, tm), 0) == 0, 1.0, 0.0)
    inv_d = 1.0 / QK_HEAD
    for h in range(N_HEADS):
        q = q_t[QK_HEAD * h:QK_HEAD * (h + 1)]
        rq = lax.rsqrt(jnp.sum(q * q, axis=0, keepdims=True) * inv_d + EPS)
        qs = q * rq * gq
        qh = jnp.concatenate([qs[:QK_NOPE], _rope_t(qs[QK_NOPE:], cos, sin), zpad], axis=0)
        qt_ref[0, HEAD_PAD * h:HEAD_PAD * (h + 1), :] = qh.astype(BF16)

        kn = kv_t[QK_NOPE * h:QK_NOPE * (h + 1)]
        rk = lax.rsqrt((jnp.sum(kn * kn, axis=0, keepdims=True) + kr_ss) * inv_d + EPS)
        kh = jnp.concatenate([kn * gk[:QK_NOPE] * rk, kr_rot * rk, zpad], axis=0)
        k_ref[0, h] = kh.T.astype(BF16)

        v0 = N_HEADS * QK_NOPE + V_HEAD * h
        vt = jnp.concatenate([kv_t[v0:v0 + V_HEAD], ones_rows], axis=0).astype(BF16)
        for c in range(tm // KV_TILE):
            v_ref[0, h, c] = vt[:, KV_TILE * c:KV_TILE * (c + 1)]


def _proj_call(x, sh, sc, p, cos_t, sin_t, mod_row, tm, kv_tokens, kv_offset, kv_buffers=None):
    b, t, d = x.shape
    off_blk = kv_offset // tm
    out_shape = (
        jax.ShapeDtypeStruct((b, N_HEADS * HEAD_PAD, t), BF16),
        jax.ShapeDtypeStruct((b, N_HEADS, kv_tokens, HEAD_PAD), BF16),
        jax.ShapeDtypeStruct((b, N_HEADS, kv_tokens // KV_TILE, V_AUG, KV_TILE), BF16),
        jax.ShapeDtypeStruct((b, t, LRU_WIDTH), F32),
        jax.ShapeDtypeStruct((b, t, LRU_WIDTH), F32),
        jax.ShapeDtypeStruct((b, t, 2 * D_MODEL), BF16),
    )
    out_specs = (
        pl.BlockSpec((1, N_HEADS * HEAD_PAD, tm), lambda bi, i: (bi, 0, i)),
        pl.BlockSpec((1, N_HEADS, tm, HEAD_PAD), lambda bi, i: (bi, 0, off_blk + i, 0)),
        pl.BlockSpec((1, N_HEADS, tm // KV_TILE, V_AUG, KV_TILE), lambda bi, i: (bi, 0, off_blk + i, 0, 0)),
        pl.BlockSpec((1, tm, LRU_WIDTH), lambda bi, i: (bi, i, 0)),
        pl.BlockSpec((1, tm, LRU_WIDTH), lambda bi, i: (bi, i, 0)),
        pl.BlockSpec((1, tm, 2 * D_MODEL), lambda bi, i: (bi, i, 0)),
    )
    consts = (p['wa'], p['wb'], p['gqa'], p['gkva'], p['wuq_t'], p['wukv_t'], p['gq'], p['gk'])
    in_specs = [
        pl.BlockSpec((1, tm, d), lambda bi, i: (bi, i, 0)),
        _mod_spec(mod_row), _mod_spec(mod_row),
        *[_const_spec(c.shape) for c in consts],
        pl.BlockSpec((2 * ROPE_PAIRS, tm), lambda bi, i: (0, i)),
        pl.BlockSpec((2 * ROPE_PAIRS, tm), lambda bi, i: (0, i)),
    ]
    args = [x, sh, sc, *consts, cos_t, sin_t]
    aliases = {}
    if kv_buffers is not None:
        aliases = {len(args): 1, len(args) + 1: 2}
        in_specs += [pl.BlockSpec(memory_space=pl.ANY), pl.BlockSpec(memory_space=pl.ANY)]
        args += list(kv_buffers)
    n_in = len(args)

    def body(*refs):
        _proj_kernel(*refs[:n_in - len(aliases)], *refs[n_in:])

    return pl.pallas_call(
        body,
        grid=(b, t // tm),
        in_specs=in_specs,
        out_specs=out_specs,
        out_shape=out_shape,
        input_output_aliases=aliases,
        compiler_params=_cparams(("arbitrary", "arbitrary")),
        name="proj",
    )(*args)


def _col_max(s):
    rows = s.shape[0]
    while rows > 4 * SUBLANES and rows % 2 == 0:
        rows //= 2
        s = jnp.maximum(s[:rows], s[rows:])
    return jnp.max(s, axis=0, keepdims=True)


def _attn_kernel(q_ref, k_ref, v_ref, o_ref, s_scr, mt_scr, *, n_steps, group, unroll):
    tq = q_ref.shape[2]
    n_blocks = tq // Q_BLOCK_LANES
    step_keys = group * KV_TILE
    q_blocks = [q_ref[0, :, Q_BLOCK_LANES * i:Q_BLOCK_LANES * (i + 1)] for i in range(n_blocks)]

    def k_step(j):
        off = j * step_keys if isinstance(j, int) else pl.multiple_of(j * step_keys, step_keys)
        return k_ref[0, 0, pl.ds(off, step_keys), :]

    def v_step(j):
        tiles = [v_ref[0, 0, group * j + t] for t in range(group)]
        return tiles[0] if group == 1 else jnp.concatenate(tiles, axis=1)

    def scores(j, slot):
        kt = k_step(j)
        for b, q_b in enumerate(q_blocks):
            s = _dot(kt, q_b)
            s_scr[slot, b] = s
            mt_scr[slot, b] = _col_max(s)

    def update(slot, j, state):
        vt = v_step(j)
        out = []
        for b, (m, acc) in enumerate(state):
            m_new = jnp.maximum(m, mt_scr[slot, b])
            p = jnp.exp2(s_scr[slot, b] - m_new).astype(BF16)
            alpha = jnp.exp2(m - m_new)
            out.append((m_new, alpha * acc + _dot(vt, p)))
        return tuple(out)

    n_pairs = (n_steps - 1) // 2
    state = tuple((jnp.full((1, Q_BLOCK_LANES), -jnp.inf, F32), jnp.zeros((V_AUG, Q_BLOCK_LANES), F32))
                  for _ in range(n_blocks))
    scores(0, 0)
    if n_pairs:
        def body(g, st):
            j = 2 * g
            scores(j + 1, 1)
            st = update(0, j, st)
            scores(j + 2, 0)
            return update(1, j + 1, st)
        state = lax.fori_loop(0, n_pairs, body, state, unroll=math.gcd(unroll, n_pairs))
    for j in range(2 * n_pairs, n_steps):
        slot = j % 2
        if j + 1 < n_steps:
            scores(j + 1, 1 - slot)
        state = update(slot, j, state)
    for i, (_, acc) in enumerate(state):
        o_ref[0, :, Q_BLOCK_LANES * i:Q_BLOCK_LANES * (i + 1)] = (
            acc[:V_HEAD] / acc[V_HEAD:V_HEAD + 1]).astype(BF16)


def _attn_call(q_t, k, v, first_tile, n_tiles, tq, group=1, unroll=1):
    b, hp, t = q_t.shape
    tk = n_tiles * KV_TILE
    assert first_tile % n_tiles == 0 and n_tiles % group == 0
    blk = first_tile // n_tiles
    n_blocks = tq // Q_BLOCK_LANES
    return pl.pallas_call(
        functools.partial(_attn_kernel, n_steps=n_tiles // group, group=group, unroll=unroll),
        grid=(b, N_HEADS, t // tq),
        in_specs=[
            pl.BlockSpec((1, HEAD_PAD, tq), lambda bi, h, i: (bi, h, i)),
            pl.BlockSpec((1, 1, tk, HEAD_PAD), lambda bi, h, i: (bi, h, blk, 0)),
            pl.BlockSpec((1, 1, n_tiles, V_AUG, KV_TILE), lambda bi, h, i: (bi, h, blk, 0, 0)),
        ],
        out_specs=pl.BlockSpec((1, V_HEAD, tq), lambda bi, h, i: (bi, h, i)),
        out_shape=jax.ShapeDtypeStruct((b, N_HEADS * V_HEAD, t), BF16),
        scratch_shapes=[
            pltpu.VMEM((2, n_blocks, group * KV_TILE, Q_BLOCK_LANES), F32),
            pltpu.VMEM((2, n_blocks, 1, Q_BLOCK_LANES), F32),
        ],
        compiler_params=_cparams(("arbitrary", "arbitrary", "arbitrary")),
        name="attn",
    )(q_t, k, v)


def _row_group_sum(p):
    rows = p.shape[0]
    while rows % (2 * SUBLANES) == 0:
        rows //= 2
        p = p[:rows] + p[rows:]
    return p


def _attn_bounded_kernel(q_ref, k_ref, v_ref, o_ref, s_scr, *, n_steps, group, tiles_per_body):
    tq = q_ref.shape[2]
    n_blocks = tq // Q_BLOCK_LANES
    step_keys = group * KV_TILE
    q_blocks = [q_ref[0, :, Q_BLOCK_LANES * i:Q_BLOCK_LANES * (i + 1)] for i in range(n_blocks)]

    def k_step(j):
        off = j * step_keys if isinstance(j, int) else pl.multiple_of(j * step_keys, step_keys)
        return k_ref[0, 0, pl.ds(off, step_keys), :]

    def scores(j):
        s_all = _dot(k_step(j), q_ref[0])
        return [s_all[:, Q_BLOCK_LANES * i:Q_BLOCK_LANES * (i + 1)] for i in range(n_blocks)]

    def accumulate(s_blocks, j, accs):
        tiles = [v_ref[0, 0, group * j + t][:V_HEAD] for t in range(group)]
        vt = tiles[0] if group == 1 else jnp.concatenate(tiles, axis=1)
        out = []
        for s, (num, den) in zip(s_blocks, accs):
            p = jnp.exp2(s)
            out.append((num + _dot(vt, p.astype(BF16)), den + _row_group_sum(p)))
        return tuple(out)

    den_rows = step_keys
    while den_rows % (2 * SUBLANES) == 0:
        den_rows //= 2
    accs = tuple((jnp.zeros((V_HEAD, Q_BLOCK_LANES), F32), jnp.zeros((den_rows, Q_BLOCK_LANES), F32))
                 for _ in range(n_blocks))
    n_bodies = (n_steps - 1) // tiles_per_body
    for b, s in enumerate(scores(0)):
        s_scr[b] = s
    if n_bodies:
        def body(g, accs):
            j0 = g * tiles_per_body
            s_cur = [s_scr[b] for b in range(n_blocks)]
            for t in range(tiles_per_body):
                s_next = scores(j0 + t + 1)
                accs = accumulate(s_cur, j0 + t, accs)
                s_cur = s_next
            for b, s in enumerate(s_cur):
                s_scr[b] = s
            return accs
        accs = lax.fori_loop(0, n_bodies, body, accs)
    s_cur = [s_scr[b] for b in range(n_blocks)]
    for j in range(n_bodies * tiles_per_body, n_steps):
        s_next = scores(j + 1) if j + 1 < n_steps else None
        accs = accumulate(s_cur, j, accs)
        s_cur = s_next
    for i, (num, den) in enumerate(accs):
        o_ref[0, :, Q_BLOCK_LANES * i:Q_BLOCK_LANES * (i + 1)] = (
            num / jnp.sum(den, axis=0, keepdims=True)).astype(BF16)


def _attn_bounded_call(q_t, k, v, n_tiles, tq, group, tiles_per_body):
    b, hp, t = q_t.shape
    n_blocks = tq // Q_BLOCK_LANES
    assert n_tiles % group == 0
    return pl.pallas_call(
        functools.partial(_attn_bounded_kernel, n_steps=n_tiles // group, group=group,
                          tiles_per_body=tiles_per_body),
        grid=(b, N_HEADS, t // tq),
        in_specs=[
            pl.BlockSpec((1, HEAD_PAD, tq), lambda bi, h, i: (bi, h, i)),
            pl.BlockSpec((1, 1, n_tiles * KV_TILE, HEAD_PAD), lambda bi, h, i: (bi, h, 0, 0)),
            pl.BlockSpec((1, 1, n_tiles, V_AUG, KV_TILE), lambda bi, h, i: (bi, h, 0, 0, 0)),
        ],
        out_specs=pl.BlockSpec((1, V_HEAD, tq), lambda bi, h, i: (bi, h, i)),
        out_shape=jax.ShapeDtypeStruct((b, N_HEADS * V_HEAD, t), BF16),
        scratch_shapes=[pltpu.VMEM((n_blocks, group * KV_TILE, Q_BLOCK_LANES), F32)],
        compiler_params=_cparams(("arbitrary", "arbitrary", "arbitrary")),
        name="attn_bounded",
    )(q_t, k, v)


def _gelu_tanh(x):
    return 0.5 * x * (1.0 + jnp.tanh(math.sqrt(2.0 / math.pi) * (x + 0.044715 * (x * x * x))))


def _short_conv(cur, prev8, next8, xbuf, cw, cb):
    chunk = cur.shape[0]
    halo = SUBLANES
    xbuf[0:halo] = prev8
    xbuf[halo:halo + chunk] = cur
    xbuf[halo + chunk:2 * halo + chunk] = next8
    xc = cb
    for k in range(CONV_W):
        o = halo - 2 + k
        xc = xc + cw[k:k + 1] * xbuf[o:o + chunk]
    return xc


def _lru_kernel(*refs, reverse, n_chunks, chunk, combine):
    width = LRU_WIDTH
    s = pl.program_id(1)
    is_ctx = s == 0
    c = jnp.clip((n_chunks - s) if reverse else (s - 1), 0, n_chunks - 1)
    if combine:
        (xcc_ref, xcl_ref, wg_ref, bg_ref, lam_ref, hbc_ref, hbl_ref, uyc_ref, uy_ref,
         outc_ref, outl_ref, a_s, b_s, h_s, carry_s) = refs
        xc = jnp.where(is_ctx, xcc_ref[0], xcl_ref[0])
    else:
        (uxc_ref, ux_ref, uxp_ref, uxn_ref, cw_ref, cb_ref, wg_ref, bg_ref, lam_ref,
         outc_ref, outl_ref, xcc_ref, xcl_ref, a_s, b_s, h_s, carry_s, xbuf) = refs
        zero_h = jnp.zeros((SUBLANES, width), F32)
        prev8 = jnp.where(is_ctx | (c == 0), zero_h, uxp_ref[0])
        next8 = jnp.where(is_ctx | (c == n_chunks - 1), zero_h, uxn_ref[0])
        cur = jnp.where(is_ctx, uxc_ref[0], ux_ref[0])
        xc = _short_conv(cur, prev8, next8, xbuf, cw_ref[...], cb_ref[...])

    xcb = xc.astype(BF16)
    half = width // 2
    d0 = _dot(xcb[:, :half], wg_ref[0])
    d1 = _dot(xcb[:, half:], wg_ref[1])
    bg = bg_ref[...]
    r = jax.nn.sigmoid(jnp.concatenate([d0[:, :half], d1[:, :half]], axis=1) + bg[:, :width])
    i = jax.nn.sigmoid(jnp.concatenate([d0[:, half:], d1[:, half:]], axis=1) + bg[:, width:])
    nl = -lam_ref[...]
    softplus = jnp.maximum(nl, 0.0) + jnp.log(1.0 + jnp.exp(-jnp.abs(nl)))
    a = jnp.exp(-RG_C * r * softplus)
    a_s[...] = a
    v = 1.0 - a * a
    b_s[...] = jnp.where(v > 0.0, v * lax.rsqrt(v), 0.0) * (i * xc)

    row = lax.broadcasted_iota(jnp.int32, (SUBLANES, width), 0)
    n_groups = chunk // SUBLANES

    def body(g, carry):
        gg = (n_groups - 1 - g) if reverse else g
        off = pl.multiple_of(gg * SUBLANES, SUBLANES)
        av = a_s[pl.ds(off, SUBLANES), :]
        bv = b_s[pl.ds(off, SUBLANES), :]
        for k in (1, 2, 4):
            shift = (SUBLANES - k) if reverse else k
            valid = (row < SUBLANES - k) if reverse else (row >= k)
            a_sh = jnp.where(valid, pltpu.roll(av, shift, 0), 1.0)
            b_sh = jnp.where(valid, pltpu.roll(bv, shift, 0), 0.0)
            bv = av * b_sh + bv
            av = av * a_sh
        hblk = av * carry + bv
        h_s[pl.ds(off, SUBLANES), :] = hblk
        return hblk[0:1] if reverse else hblk[SUBLANES - 1:SUBLANES]

    carry0 = jnp.where(is_ctx, jnp.zeros((1, width), F32), carry_s[...])
    carry_s[...] = lax.fori_loop(0, n_groups, body, carry0)
    h = h_s[...]

    if combine:
        hb = jnp.where(is_ctx, hbc_ref[0], hbl_ref[0])
        uy = jnp.where(is_ctx, uyc_ref[0], uy_ref[0])
        val = ((h + hb) * _gelu_tanh(uy)).astype(BF16)
    else:
        val = h

    @pl.when(is_ctx)
    def _():
        outc_ref[0] = val
        if not combine:
            xcc_ref[0] = xc

    @pl.when(jnp.logical_not(is_ctx))
    def _():
        outl_ref[0] = val
        if not combine:
            xcl_ref[0] = xc


def _lru_call(ux_c, ux_l, p, direction, reverse, combine_with=None):
    b, s_len, w = ux_l.shape
    c_len = ux_c.shape[1]
    chunk = c_len
    assert s_len % chunk == 0 and chunk % SUBLANES == 0
    n = s_len // chunk
    hpc = chunk // SUBLANES
    n_hblk = s_len // SUBLANES

    def cidx(s):
        return jnp.clip((n - s) if reverse else (s - 1), 0, n - 1)

    ctx_spec = pl.BlockSpec((1, c_len, w), lambda bi, s: (bi, 0, 0))
    lat_spec = pl.BlockSpec((1, chunk, w), lambda bi, s: (bi, cidx(s), 0))
    gate_consts = (p['wg'][direction], p['bg'][direction], p['lam'][direction])
    combine = combine_with is not None
    if combine:
        consts = gate_consts
        in_specs = [ctx_spec, lat_spec, *[_const_spec(cst.shape) for cst in consts],
                    ctx_spec, lat_spec, ctx_spec, lat_spec]
        args = [ux_c, ux_l, *consts, *combine_with]
        out_specs = (ctx_spec, lat_spec)
        out_shape = (jax.ShapeDtypeStruct(ux_c.shape, BF16), jax.ShapeDtypeStruct(ux_l.shape, BF16))
    else:
        consts = (p['conv_w'], p['conv_b'], *gate_consts)
        in_specs = [
            ctx_spec, lat_spec,
            pl.BlockSpec((1, SUBLANES, w), lambda bi, s: (bi, jnp.maximum(cidx(s) * hpc - 1, 0), 0)),
            pl.BlockSpec((1, SUBLANES, w), lambda bi, s: (bi, jnp.minimum((cidx(s) + 1) * hpc, n_hblk - 1), 0)),
            *[_const_spec(cst.shape) for cst in consts],
        ]
        args = [ux_c, ux_l, ux_l, ux_l, *consts]
        out_specs = (ctx_spec, lat_spec, ctx_spec, lat_spec)
        out_shape = tuple(jax.ShapeDtypeStruct(a.shape, F32) for a in (ux_c, ux_l, ux_c, ux_l))
    scratch = [
        pltpu.VMEM((chunk, w), F32), pltpu.VMEM((chunk, w), F32), pltpu.VMEM((chunk, w), F32),
        pltpu.VMEM((1, w), F32),
    ]
    if not combine:
        scratch.append(pltpu.VMEM((chunk + 2 * SUBLANES, w), F32))
    return pl.pallas_call(
        functools.partial(_lru_kernel, reverse=reverse, n_chunks=n, chunk=chunk, combine=combine),
        grid=(b, n + 1),
        in_specs=in_specs,
        out_specs=out_specs,
        out_shape=out_shape,
        scratch_shapes=scratch,
        compiler_params=_cparams(("arbitrary", "arbitrary")),
        name="lru_fwd" if combine else "lru_bwd",
    )(*args)


def _merge_kernel(x_ref, att_ref, lru_ref, gs_ref, g_ref, woa_ref, wol_ref, wout_ref, o_ref):
    att = lax.dot_general(att_ref[0], woa_ref[...], TN_DIMS, preferred_element_type=F32)
    lru = _dot(lru_ref[0], wol_ref[...])
    gs = gs_ref[0].astype(F32)
    merged = (gs[:, :D_MODEL] * att + gs[:, D_MODEL:] * lru).astype(BF16)
    o_ref[0] = x_ref[0] + g_ref[0] * _dot(merged, wout_ref[...])


def _merge_call(x, att_t, lru, gs, g, p, mod_row, tm):
    b, t, d = x.shape
    consts = (p['w_o_attn'], p['w_o_lru'], p['w_out'])
    return pl.pallas_call(
        _merge_kernel,
        grid=(b, t // tm),
        in_specs=[
            pl.BlockSpec((1, tm, d), lambda bi, i: (bi, i, 0)),
            pl.BlockSpec((1, N_HEADS * V_HEAD, tm), lambda bi, i: (bi, 0, i)),
            pl.BlockSpec((1, tm, LRU_WIDTH), lambda bi, i: (bi, i, 0)),
            pl.BlockSpec((1, tm, 2 * D_MODEL), lambda bi, i: (bi, i, 0)),
            _mod_spec(mod_row),
            *[_const_spec(c.shape) for c in consts],
        ],
        out_specs=pl.BlockSpec((1, tm, d), lambda bi, i: (bi, i, 0)),
        out_shape=jax.ShapeDtypeStruct(x.shape, F32),
        compiler_params=_cparams(("arbitrary", "arbitrary")),
        name="merge",
    )(x, att_t, lru, gs, g, *consts)


def _block_diag_gates(w_r, w_i):
    def bd(w4):
        z = jnp.zeros((4, LRU_BLOCK, 4, LRU_BLOCK), w4.dtype)
        idx = jnp.arange(4)
        z = z.at[idx, :, idx, :].set(w4)
        return z.reshape(4 * LRU_BLOCK, 4 * LRU_BLOCK)
    groups = []
    for j in range(2):
        groups.append(jnp.concatenate([bd(w_r[4 * j:4 * j + 4]), bd(w_i[4 * j:4 * j + 4])], axis=1))
    return jnp.stack(groups).astype(BF16)


def _layer_params(l, w):
    w_in = w['w_in'][l]
    split = Q_LORA + KV_LORA + QK_ROPE
    wa = jnp.pad(w_in[:, :split], ((0, 0), (0, PROJ_A - split))).astype(BF16)
    wb = w_in[:, split:].astype(BF16)
    w_ukv = w['w_ukv'][l].reshape(KV_LORA, N_HEADS, QK_NOPE + V_HEAD)
    wukv_t = jnp.concatenate([
        w_ukv[:, :, :QK_NOPE].reshape(KV_LORA, N_HEADS * QK_NOPE),
        w_ukv[:, :, QK_NOPE:].reshape(KV_LORA, N_HEADS * V_HEAD)], axis=1).T.astype(BF16)
    return {
        'ff1_w_in': w['ff1_w_in'][l].astype(BF16), 'ff1_w_out': w['ff1_w_out'][l].astype(BF16),
        'ff2_w_in': w['ff2_w_in'][l].astype(BF16), 'ff2_w_out': w['ff2_w_out'][l].astype(BF16),
        'wa': wa, 'wb': wb,
        'gqa': w['q_a_norm'][l].reshape(1, Q_LORA), 'gkva': w['kv_a_norm'][l].reshape(1, KV_LORA),
        'wuq_t': w['w_uq'][l].T.astype(BF16), 'wukv_t': wukv_t,
        'gq': (w['q_norm'][l] * (ATTN_SCALE * math.log2(math.e))).reshape(QK_HEAD, 1),
        'gk': w['k_norm'][l].reshape(QK_HEAD, 1),
        'conv_w': w['conv_w'][l], 'conv_b': w['conv_b'][l].reshape(1, LRU_WIDTH),
        'wg': [_block_diag_gates(w['w_rgate'][l, d], w['w_igate'][l, d]) for d in range(2)],
        'bg': [jnp.concatenate([w['b_rgate'][l, d], w['b_igate'][l, d]]).reshape(1, 2 * LRU_WIDTH)
               for d in range(2)],
        'lam': [w['lru_lambda'][l, d].reshape(1, LRU_WIDTH) for d in range(2)],
        'w_o_attn': w['w_o_attn'][l].astype(BF16), 'w_o_lru': w['w_o_lru'][l].astype(BF16),
        'w_out': w['w_out'][l].astype(BF16),
    }


def _rope_tables_t(n_tokens):
    rows = n_tokens // GRID_W
    row = jnp.repeat(jnp.arange(rows, dtype=F32), GRID_W)
    col = jnp.tile(jnp.arange(GRID_W, dtype=F32), rows)
    inv_freq = ROPE_BASE ** (-jnp.arange(ROPE_PAIRS, dtype=F32) / ROPE_PAIRS)
    ang = jnp.concatenate([inv_freq[:, None] * row[None, :], inv_freq[:, None] * col[None, :]], axis=0)
    return jnp.cos(ang), jnp.sin(ang)


def kernel(x, c, ctx, c_ctx, w_ada, b_ada, ff1_w_in, ff1_w_out, ff2_w_in, ff2_w_out, w_in, q_a_norm, w_uq,
           kv_a_norm, w_ukv, q_norm, k_norm, conv_w, conv_b, lru_lambda, w_rgate, b_rgate, w_igate, b_igate,
           w_o_attn, w_o_lru, w_out):
    weights = dict(ff1_w_in=ff1_w_in, ff1_w_out=ff1_w_out, ff2_w_in=ff2_w_in, ff2_w_out=ff2_w_out, w_in=w_in,
                   q_a_norm=q_a_norm, w_uq=w_uq, kv_a_norm=kv_a_norm, w_ukv=w_ukv, q_norm=q_norm, k_norm=k_norm,
                   conv_w=conv_w, conv_b=conv_b, lru_lambda=lru_lambda, w_rgate=w_rgate, b_rgate=b_rgate,
                   w_igate=w_igate, b_igate=b_igate, w_o_attn=w_o_attn, w_o_lru=w_o_lru, w_out=w_out)
    batch, seq, d = x.shape
    c_len = ctx.shape[1]
    depth = w_ada.shape[0]
    assert d == D_MODEL and c_len == KV_TILE and seq % KV_TILE == 0 and batch < SUBLANES
    ctx_row = batch
    tm = 256
    tm_ffn = min(512, seq)
    tq = min(1024, seq)
    assert seq % tq == 0 and seq % tm_ffn == 0

    cond_raw = jnp.zeros((SUBLANES, d), F32).at[:batch].set(c).at[ctx_row].set(c_ctx)
    mods = _ada_call(cond_raw, w_ada, b_ada)
    mods = mods.reshape(depth, SUBLANES, N_MOD, 1, d).transpose(0, 2, 1, 3, 4)

    cos_l, sin_l = _rope_tables_t(seq)
    cos_c = jnp.ones((2 * ROPE_PAIRS, c_len), F32)
    sin_c = jnp.zeros((2 * ROPE_PAIRS, c_len), F32)

    kv_buffers = (jnp.zeros((batch, N_HEADS, seq + c_len, HEAD_PAD), BF16),
                  jnp.zeros((batch, N_HEADS, (seq + c_len) // KV_TILE, V_AUG, KV_TILE), BF16))
    xl, xc = x, ctx
    for l in range(depth):
        p = _layer_params(l, weights)
        m = mods[l]
        last = l == depth - 1
        xl = _ffn_call(xl, m[0], m[1], m[2], p['ff1_w_in'], p['ff1_w_out'], None, tm_ffn)
        xc = _ffn_call(xc, m[0], m[1], m[2], p['ff1_w_in'], p['ff1_w_out'], ctx_row, c_len)
        kv_tokens = seq + c_len
        qt_l, k_all, v_all, ux_l, uy_l, gs_l = _proj_call(xl, m[3], m[4], p, cos_l, sin_l, None, tm, kv_tokens, 0,
                                                          kv_buffers=kv_buffers)
        qt_c, k_all, v_all, ux_c, uy_c, gs_c = _proj_call(xc, m[3], m[4], p, cos_c, sin_c, ctx_row, c_len,
                                                          kv_tokens, seq, kv_buffers=(k_all, v_all))
        kv_buffers = (k_all, v_all)
        n_kv_tiles = kv_tokens // KV_TILE
        group = ATTN_GROUP if n_kv_tiles % ATTN_GROUP == 0 else 1
        score_bound = (QK_HEAD * ATTN_SCALE * math.log2(math.e) * 1.01
                       * jnp.max(jnp.abs(q_norm[l])) * jnp.max(jnp.abs(k_norm[l])))
        att_l = lax.cond(
            score_bound <= SCORE_BOUND_LOG2,
            lambda: _attn_bounded_call(qt_l, k_all, v_all, n_kv_tiles, tq,
                                       ATTN_BOUNDED_GROUP if n_kv_tiles % ATTN_BOUNDED_GROUP == 0 else 1,
                                       ATTN_TILES_PER_BODY),
            lambda: _attn_call(qt_l, k_all, v_all, 0, n_kv_tiles, tq, group=group, unroll=ATTN_PAIR_UNROLL))
        hb_c, hb_l, xv_c, xv_l = _lru_call(ux_c, ux_l, p, 1, True)
        lru_c, lru_l = _lru_call(xv_c, xv_l, p, 0, False, combine_with=(hb_c, hb_l, uy_c, uy_l))
        xl = _merge_call(xl, att_l, lru_l, gs_l, m[5], p, None, tm_ffn)
        xl = _ffn_call(xl, m[6], m[7], m[8], p['ff2_w_in'], p['ff2_w_out'], None, tm_ffn)
        if not last:
            att_c = _attn_call(qt_c, k_all, v_all, seq // KV_TILE, c_len // KV_TILE, c_len)
            xc = _merge_call(xc, att_c, lru_c, gs_c, m[5], p, ctx_row, c_len)
            xc = _ffn_call(xc, m[6], m[7], m[8], p['ff2_w_in'], p['ff2_w_out'], ctx_row, c_len)
    return xl
```
